```python
import math
import jax, jax.numpy as jnp
from jax import lax
import numpy as np

D_MODEL = 4096
BATCH = 4
SEQ = 2048
DEPTH = 4
DEC_BATCH = 128
DEC_SEQ = 8
PAST_LEN = 16384
PAGE_SIZE = 128

N_EVEN = (DEPTH + 1) // 2
N_ODD = DEPTH // 2
POOL_WINDOWS = (2, 4, 8, 16)
N_POOL_GROUPS = len(POOL_WINDOWS)
POOL_DIM = D_MODEL // 2
POOL_GROUP_DIM = POOL_DIM // N_POOL_GROUPS
POOL_BUF = max(POOL_WINDOWS) - 1
SSM_INNER = 3 * D_MODEL // 2
SSM_HEAD_DIM = 64
SSM_HEADS = SSM_INNER // SSM_HEAD_DIM
SSM_GROUPS = 8
SSM_STATE = 128
SSM_CONV = 4
SSM_CHUNK = 128
SSM_GN = SSM_GROUPS * SSM_STATE
CONV_DIM = SSM_INNER + 2 * SSM_GN
EVEN_IN = POOL_DIM + SSM_INNER + CONV_DIM + SSM_HEADS
EVEN_MIX = POOL_DIM + SSM_INNER
CMLP_CHUNK = 128
CMLP_DIM = D_MODEL
CMLP_HEADS = 8
CMLP_HEAD_DIM = CMLP_DIM // CMLP_HEADS
MOE_GROUPS = 4
MOE_PER_GROUP = 8
MOE_EXPERTS = MOE_GROUPS * MOE_PER_GROUP
MOE_TOPK = 2
EXPERT_DIM = D_MODEL // 4
MOE_BLOCK = 128
DN_ALPHA = (2 * DEPTH) ** 0.25
DN_BETA = (8 * DEPTH) ** -0.25
LN_EPS = 1e-5
RMS_EPS = 1e-5

kernel_name = 'hybrid_pool_ssd_chunkmlp_hmoe_step'


def layer_norm(x, g, b):
    xf = x.astype(jnp.float32)
    mu = jnp.mean(xf, -1, keepdims=True)
    var = jnp.mean(jnp.square(xf - mu), -1, keepdims=True)
    return ((xf - mu) * lax.rsqrt(var + LN_EPS) * g + b).astype(x.dtype)


def pool_mixer(u, buf, start_pos, w_grp, scale):
    b_sz, L, _ = u.shape
    ext = jnp.concatenate([buf, u], axis=1)
    cs = jnp.pad(jnp.cumsum(ext.astype(jnp.float32), axis=1), ((0, 0), (1, 0), (0, 0)))
    end = cs[:, POOL_BUF + 1:POOL_BUF + 1 + L]
    pos = start_pos + jnp.arange(L, dtype=jnp.int32)
    outs = []
    for gi, w in enumerate(POOL_WINDOWS):
        sl = slice(gi * POOL_GROUP_DIM, (gi + 1) * POOL_GROUP_DIM)
        start = cs[:, POOL_BUF + 1 - w:POOL_BUF + 1 - w + L, sl]
        cnt = jnp.minimum(pos + 1, w).astype(jnp.float32)[None, :, None]
        outs.append((end[..., sl] - start) / cnt)
    pooled = jnp.stack(outs, axis=2)
    mixed = pooled.astype(u.dtype) - u.reshape(b_sz, L, N_POOL_GROUPS, POOL_GROUP_DIM)
    y = jnp.einsum('blgc,gcd->blgd', mixed, w_grp).reshape(b_sz, L, POOL_DIM)
    return y * scale, ext[:, -POOL_BUF:]


def causal_conv(xbc, buf, w, b):
    L = xbc.shape[1]
    ext = jnp.concatenate([buf, xbc], axis=1)
    out = sum(ext[:, k:k + L] * w[k] for k in range(SSM_CONV)) + b
    return jax.nn.silu(out), ext[:, -(SSM_CONV - 1):]


def ssd_scan(x, dt, a, bm, cm, h0):
    f32 = jnp.float32
    b_sz, L = x.shape[0], x.shape[1]
    q = math.gcd(L, SSM_CHUNK)
    nc = L // q
    G, R, P, N = SSM_GROUPS, SSM_HEADS // SSM_GROUPS, SSM_HEAD_DIM, SSM_STATE
    xdt = (x.astype(f32) * dt[..., None]).reshape(b_sz, nc, q, G, R, P)
    acs = jnp.cumsum((dt * a).reshape(b_sz, nc, q, G, R), axis=2)
    bq = bm.astype(f32).reshape(b_sz, nc, q, G, N)
    cq = cm.astype(f32).reshape(b_sz, nc, q, G, N)
    acs_t = jnp.moveaxis(acs, 2, -1)
    causal = jnp.tril(jnp.ones((q, q), dtype=bool))
    decay = jnp.exp(jnp.where(causal, acs_t[..., :, None] - acs_t[..., None, :], -jnp.inf))
    scores = jnp.einsum('bclgn,bcsgn->bcgls', cq, bq)
    y_diag = jnp.einsum('bcgrls,bcsgrp->bclgrp', scores[:, :, :, None] * decay, xdt)
    to_end = jnp.exp(acs[:, :, -1:] - acs)
    chunk_st = jnp.einsum('bclgn,bclgr,bclgrp->bcgrpn', bq, to_end, xdt)
    chunk_decay = jnp.exp(acs[:, :, -1])

    def step(h, inp):
        st, dec = inp
        return h * dec[..., None, None] + st, h

    h_last, h_in = lax.scan(step, h0.astype(f32).reshape(b_sz, G, R, P, N),
                            (jnp.moveaxis(chunk_st, 1, 0), jnp.moveaxis(chunk_decay, 1, 0)))
    h_in = jnp.moveaxis(h_in, 0, 1)
    y_off = jnp.einsum('bclgn,bcgrpn->bclgrp', cq, h_in) * jnp.exp(acs)[..., None]
    y = (y_diag + y_off).reshape(b_sz, L, SSM_HEADS, P)
    return y, h_last.reshape(b_sz, SSM_HEADS, P, N)


def even_mixer(h, pool_buf, conv_buf, ssm_h0, start_pos, w_in, pool_w, pool_scale, conv_w, conv_b,
               dt_bias, a_log, d_skip, norm_w, w_out):
    f32 = jnp.float32
    b_sz, L, _ = h.shape
    proj = h @ w_in
    s1 = POOL_DIM
    s2 = s1 + SSM_INNER
    s3 = s2 + CONV_DIM
    u_a, z, xbc, dt_raw = proj[..., :s1], proj[..., s1:s2], proj[..., s2:s3], proj[..., s3:]
    y_a, new_pool = pool_mixer(u_a, pool_buf, start_pos, pool_w, pool_scale)
    xbc, new_conv = causal_conv(xbc, conv_buf, conv_w, conv_b)
    xs = xbc[..., :SSM_INNER].reshape(b_sz, L, SSM_HEADS, SSM_HEAD_DIM)
    bm = xbc[..., SSM_INNER:SSM_INNER + SSM_GN].reshape(b_sz, L, SSM_GROUPS, SSM_STATE)
    cm = xbc[..., SSM_INNER + SSM_GN:].reshape(b_sz, L, SSM_GROUPS, SSM_STATE)
    dt = jax.nn.softplus(dt_raw.astype(f32) + dt_bias.astype(f32))
    a = -jnp.exp(a_log.astype(f32))
    y, h_last = ssd_scan(xs, dt, a, bm, cm, ssm_h0)
    y = y + xs.astype(f32) * d_skip.astype(f32)[:, None]
    y = y.reshape(b_sz, L, SSM_INNER) * jax.nn.silu(z.astype(f32))
    yg = y.reshape(b_sz, L, SSM_GROUPS, SSM_INNER // SSM_GROUPS)
    yg = yg * lax.rsqrt(jnp.mean(jnp.square(yg), -1, keepdims=True) + RMS_EPS)
    y_b = (yg.reshape(b_sz, L, SSM_INNER) * norm_w).astype(h.dtype)
    out = jnp.concatenate([y_a.astype(h.dtype), y_b], axis=-1) @ w_out
    return out, new_pool, new_conv, h_last.astype(ssm_h0.dtype)


def cmlp_mixer(h, w_in, b_in, v_g, v_b, w_s, b_s, w_out):
    b_sz, L, _ = h.shape
    zz = jax.nn.gelu(h @ w_in + b_in)
    u, v = zz[..., :CMLP_DIM], zz[..., CMLP_DIM:]
    v = layer_norm(v, v_g, v_b)
    q = min(L, CMLP_CHUNK)
    nc = L // q
    ws = jnp.where(jnp.tril(jnp.ones((q, q), dtype=bool)), w_s[:, :q, :q], 0.0).astype(v.dtype)
    vq = v.reshape(b_sz, nc, q, CMLP_HEADS, CMLP_HEAD_DIM)
    mixed = jnp.einsum('hts,bcshd->bcthd', ws, vq) + b_s[:, :q].T[None, None, :, :, None]
    out = (u * mixed.reshape(b_sz, L, CMLP_DIM).astype(u.dtype)) @ w_out
    return out, v


def grouped_experts(x, e_idx, e_w, w_gate, w_up, w_down):
    T, D = x.shape
    n_assign = T * MOE_TOPK
    n_blocks = -(-n_assign // MOE_BLOCK) + MOE_EXPERTS
    n_rows = n_blocks * MOE_BLOCK
    flat_e = e_idx.reshape(-1).astype(jnp.int32)
    order = jnp.argsort(flat_e).astype(jnp.int32)
    sorted_e = flat_e[order]
    counts = jnp.zeros((MOE_EXPERTS,), jnp.int32).at[flat_e].add(1)
    padded = (counts + MOE_BLOCK - 1) // MOE_BLOCK * MOE_BLOCK
    pad_end = jnp.cumsum(padded)
    pad_start = pad_end - padded
    start = jnp.cumsum(counts) - counts
    dest = pad_start[sorted_e] + jnp.arange(n_assign, dtype=jnp.int32) - start[sorted_e]
    row_tok = jnp.zeros((n_rows,), jnp.int32).at[dest].set(order // MOE_TOPK)
    row_w = jnp.zeros((n_rows,), jnp.float32).at[dest].set(e_w.reshape(-1)[order])
    block_e = jnp.minimum(jnp.searchsorted(pad_end, jnp.arange(n_blocks, dtype=jnp.int32) * MOE_BLOCK, side='right'),
                          MOE_EXPERTS - 1)

    def run_block(args):
        tok, e = args
        xb = x[tok]
        hb = jax.nn.silu(xb @ w_gate[e]) * (xb @ w_up[e])
        return hb @ w_down[e]

    yb = lax.map(run_block, (row_tok.reshape(n_blocks, MOE_BLOCK), block_e))
    y = jnp.zeros((T, D), jnp.float32).at[row_tok].add(yb.reshape(n_rows, D).astype(jnp.float32) * row_w[:, None])
    return y.astype(x.dtype)


def hier_moe(h, w_group, w_expert, w_gate, w_up, w_down):
    b_sz, L, D = h.shape
    x = h.reshape(-1, D)
    T = x.shape[0]
    g_logits = (x @ w_group).astype(jnp.float32)
    g_prob = jax.nn.softmax(g_logits, axis=-1)
    g_sel = jnp.argmax(g_logits, axis=-1)
    g_w = jnp.take_along_axis(g_prob, g_sel[:, None], axis=-1)
    e_logits = (x @ w_expert).astype(jnp.float32).reshape(T, MOE_GROUPS, MOE_PER_GROUP)
    e_logits = jnp.take_along_axis(e_logits, g_sel[:, None, None], axis=1)[:, 0]
    top_v, top_i = lax.top_k(e_logits, MOE_TOPK)
    e_w = jax.nn.softmax(top_v, axis=-1) * g_w
    e_idx = g_sel[:, None] * MOE_PER_GROUP + top_i
    return grouped_experts(x, e_idx, e_w, w_gate, w_up, w_down).reshape(b_sz, L, D)


def setup_inputs(seed: int = 0) -> dict:
    key = jax.random.key(seed)
    ks = jax.random.split(key, 32)
    f32 = jnp.float32
    nrm = lambda k, shape, s: jax.random.normal(k, shape, f32) * s
    dt0 = jnp.exp(jax.random.uniform(ks[10], (N_EVEN, SSM_HEADS), f32) * (math.log(0.1) - math.log(0.001)) + math.log(0.001))
    return {
        'x_prompt': nrm(ks[0], (BATCH, SEQ, D_MODEL), 1.0),
        'x_sample': nrm(ks[1], (DEC_BATCH, DEC_SEQ, D_MODEL), 1.0),
        'state_pool': nrm(ks[2], (N_EVEN, DEC_BATCH, POOL_BUF, POOL_DIM), 1.0),
        'state_conv': nrm(ks[3], (N_EVEN, DEC_BATCH, SSM_CONV - 1, CONV_DIM), 1.0),
        'state_ssm': nrm(ks[4], (N_EVEN, DEC_BATCH, SSM_HEADS, SSM_HEAD_DIM, SSM_STATE), 0.1),
        'w_in_even': nrm(ks[5], (N_EVEN, D_MODEL, EVEN_IN), D_MODEL ** -0.5),
        'pool_w': nrm(ks[6], (N_EVEN, N_POOL_GROUPS, POOL_GROUP_DIM, POOL_GROUP_DIM), POOL_GROUP_DIM ** -0.5),
        'pool_scale': 1.0 + nrm(ks[7], (N_EVEN, POOL_DIM), 0.1),
        'conv_w': nrm(ks[8], (N_EVEN, SSM_CONV, CONV_DIM), SSM_CONV ** -0.5),
        'conv_b': nrm(ks[9], (N_EVEN, CONV_DIM), 0.1),
        'dt_bias': dt0 + jnp.log(-jnp.expm1(-dt0)),
        'a_log': jnp.log(jax.random.uniform(ks[11], (N_EVEN, SSM_HEADS), f32, 1.0, 16.0)),
        'd_skip': 1.0 + nrm(ks[12], (N_EVEN, SSM_HEADS), 0.1),
        'ssm_norm_w': 1.0 + nrm(ks[13], (N_EVEN, SSM_INNER), 0.1),
        'w_out_even': nrm(ks[14], (N_EVEN, EVEN_MIX, D_MODEL), EVEN_MIX ** -0.5 * DN_BETA),
        'cmlp_w_in': nrm(ks[15], (N_ODD, D_MODEL, 2 * CMLP_DIM), D_MODEL ** -0.5),
        'cmlp_b_in': nrm(ks[16], (N_ODD, 2 * CMLP_DIM), 0.02),
        'cmlp_v_g': 1.0 + nrm(ks[17], (N_ODD, CMLP_DIM), 0.1),
        'cmlp_v_b': nrm(ks[18], (N_ODD, CMLP_DIM), 0.02),
        'cmlp_w_s': nrm(ks[19], (N_ODD, CMLP_HEADS, CMLP_CHUNK, CMLP_CHUNK), CMLP_CHUNK ** -0.5),
        'cmlp_b_s': 1.0 + nrm(ks[20], (N_ODD, CMLP_HEADS, CMLP_CHUNK), 0.1),
        'cmlp_w_out': nrm(ks[21], (N_ODD, CMLP_DIM, D_MODEL), CMLP_DIM ** -0.5 * DN_BETA),
        'ln_mix_g': 1.0 + nrm(ks[22], (DEPTH, D_MODEL), 0.1),
        'ln_mix_b': nrm(ks[23], (DEPTH, D_MODEL), 0.02),
        'ln_ffn_g': 1.0 + nrm(ks[24], (DEPTH, D_MODEL), 0.1),
        'ln_ffn_b': nrm(ks[25], (DEPTH, D_MODEL), 0.02),
        'moe_w_group': nrm(ks[26], (DEPTH, D_MODEL, MOE_GROUPS), D_MODEL ** -0.5),
        'moe_w_expert': nrm(ks[27], (DEPTH, D_MODEL, MOE_EXPERTS), D_MODEL ** -0.5),
        'moe_w_gate': nrm(ks[28], (DEPTH, MOE_EXPERTS, D_MODEL, EXPERT_DIM), D_MODEL ** -0.5),
        'moe_w_up': nrm(ks[29], (DEPTH, MOE_EXPERTS, D_MODEL, EXPERT_DIM), D_MODEL ** -0.5),
        'moe_w_down': nrm(ks[30], (DEPTH, MOE_EXPERTS, EXPERT_DIM, D_MODEL), EXPERT_DIM ** -0.5 * DN_BETA),
    }


def reference(x_prompt, x_sample, state_pool, state_conv, state_ssm, w_in_even, pool_w, pool_scale, conv_w, conv_b,
              dt_bias, a_log, d_skip, ssm_norm_w, w_out_even, cmlp_w_in, cmlp_b_in, cmlp_v_g, cmlp_v_b, cmlp_w_s,
              cmlp_b_s, cmlp_w_out, ln_mix_g, ln_mix_b, ln_ffn_g, ln_ffn_b, moe_w_group, moe_w_expert, moe_w_gate,
              moe_w_up, moe_w_down):

    def trunk(x, pool_in, conv_in, ssm_in, start_pos):
        new_pool, new_conv, new_ssm, new_v = [], [], [], []
        for layer in range(DEPTH):
            i = layer // 2
            if layer % 2 == 0:
                mix, pb, cb, sb = even_mixer(x, pool_in[i], conv_in[i], ssm_in[i], start_pos, w_in_even[i], pool_w[i],
                                             pool_scale[i], conv_w[i], conv_b[i], dt_bias[i], a_log[i], d_skip[i],
                                             ssm_norm_w[i], w_out_even[i])
                new_pool.append(pb)
                new_conv.append(cb)
                new_ssm.append(sb)
            else:
                mix, v = cmlp_mixer(x, cmlp_w_in[i], cmlp_b_in[i], cmlp_v_g[i], cmlp_v_b[i], cmlp_w_s[i], cmlp_b_s[i],
                                    cmlp_w_out[i])
                new_v.append(v)
            x = layer_norm(DN_ALPHA * x + mix, ln_mix_g[layer], ln_mix_b[layer])
            ffn = hier_moe(x, moe_w_group[layer], moe_w_expert[layer], moe_w_gate[layer], moe_w_up[layer],
                           moe_w_down[layer])
            x = layer_norm(DN_ALPHA * x + ffn, ln_ffn_g[layer], ln_ffn_b[layer])
        return x, jnp.stack(new_pool), jnp.stack(new_conv), jnp.stack(new_ssm), jnp.stack(new_v)

    b_p = x_prompt.shape[0]
    dt_ = x_prompt.dtype
    zero_pool = jnp.zeros((N_EVEN, b_p, POOL_BUF, POOL_DIM), dt_)
    zero_conv = jnp.zeros((N_EVEN, b_p, SSM_CONV - 1, CONV_DIM), dt_)
    zero_ssm = jnp.zeros((N_EVEN, b_p, SSM_HEADS, SSM_HEAD_DIM, SSM_STATE), state_ssm.dtype)
    y_prompt, pool_p, conv_p, ssm_p, _ = trunk(x_prompt, zero_pool, zero_conv, zero_ssm, 0)
    y_sample, pool_s, conv_s, ssm_s, v_s = trunk(x_sample, state_pool, state_conv, state_ssm, PAST_LEN)
    return (y_prompt, y_sample, pool_p, pool_s, conv_p, conv_s, ssm_p, ssm_s, v_s)
```

```python
import functools
import math

import jax
import jax.numpy as jnp
from jax import lax
from jax.experimental import pallas as pl
from jax.experimental.pallas import tpu as pltpu

F32 = jnp.float32
BF16 = jnp.bfloat16
U32 = jnp.uint32
I32 = jnp.int32

POOL_WINDOWS = (2, 4, 8, 16)
POOL_BUF = max(POOL_WINDOWS) - 1
SSM_HEAD_DIM = 64
SSM_GROUPS = 8
SEQ_CHUNK = 128
MOE_TOPK = 2
PAST_LEN = 16384
LN_EPS = 1e-5
RMS_EPS = 1e-5

LANE = 128
SUBLANE = 8
V7X_VMEM_BYTES = 64 * 1024 * 1024
VMEM_LIMIT = V7X_VMEM_BYTES - 8 * 1024 * 1024
TILE = 128
MOE_BLOCK = 256

_NT = (((1,), (1,)), ((), ()))


def _cparams(sem):
    return pltpu.CompilerParams(dimension_semantics=sem, vmem_limit_bytes=VMEM_LIMIT)


def _sigmoid(x):
    return 1.0 / (1.0 + jnp.exp(-x))


def _silu(x):
    return x * _sigmoid(x)


def _gelu_tanh(x):
    return 0.5 * x * (1.0 + jnp.tanh(math.sqrt(2.0 / math.pi) * (x + 0.044715 * (x * x * x))))


def _softplus(x):
    return jnp.maximum(x, 0.0) + jnp.log1p(jnp.exp(-jnp.abs(x)))


def _layer_norm(y, g, b):
    mu = jnp.mean(y, axis=-1, keepdims=True)
    d = y - mu
    var = jnp.mean(d * d, axis=-1, keepdims=True)
    return d * lax.rsqrt(var + LN_EPS) * g + b


def _mm_body(*refs, n_chunks, has_bias, has_scale, act):
    xs = refs[:n_chunks]
    ws = refs[n_chunks:2 * n_chunks]
    pos = 2 * n_chunks
    bias_ref = scale_ref = None
    if has_bias:
        bias_ref = refs[pos]
        pos += 1
    if has_scale:
        scale_ref = refs[pos]
        pos += 1
    o_ref, wb = refs[pos], refs[pos + 1]

    @pl.when(pl.program_id(1) == 0)
    def _():
        for c in range(n_chunks):
            wb[c] = ws[c][...].astype(BF16)

    acc = None
    for c in range(n_chunks):
        d = jnp.dot(xs[c][...], wb[c], preferred_element_type=F32)
        acc = d if acc is None else acc + d
    if has_scale:
        acc = acc * scale_ref[...]
    if has_bias:
        acc = acc + bias_ref[...]
    if act == "gelu":
        acc = _gelu_tanh(acc)
    o_ref[...] = acc.astype(o_ref.dtype)


def _matmul(lhs, w, w_lead, w_row_blocks, kc, col0, n_cols, tn, tm, out_dtype, bias=None, act=None, name="mm"):
    n_chunks = len(lhs)
    m = lhs[0][0].shape[0]
    assert m % tm == 0 and n_cols % tn == 0 and col0 % tn == 0
    cb0 = col0 // tn
    nl = len(w_lead)
    in_specs = []
    for _, cb in lhs:
        in_specs.append(pl.BlockSpec((tm, kc), functools.partial(lambda j, i, cb: (i, cb), cb=cb)))
    for rb in w_row_blocks:
        in_specs.append(pl.BlockSpec((None,) * nl + (kc, tn),
                                     functools.partial(lambda j, i, rb: tuple(w_lead) + (rb, cb0 + j), rb=rb)))
    args = [a for a, _ in lhs] + [w] * n_chunks
    if bias is not None:
        in_specs.append(pl.BlockSpec((1, tn), lambda j, i: (0, j)))
        args.append(bias)
    return pl.pallas_call(
        functools.partial(_mm_body, n_chunks=n_chunks, has_bias=bias is not None, has_scale=False, act=act),
        out_shape=jax.ShapeDtypeStruct((m, n_cols), out_dtype),
        grid=(n_cols // tn, m // tm),
        in_specs=in_specs,
        out_specs=pl.BlockSpec((tm, tn), lambda j, i: (i, j)),
        scratch_shapes=[pltpu.VMEM((n_chunks, kc, tn), BF16)],
        compiler_params=_cparams(("arbitrary", "arbitrary")),
        name=name,
    )(*args)


def _group_matmul(x, w, layer, scale, tm):
    m = x.shape[0]
    _, n_g, gd, _ = w.shape
    return pl.pallas_call(
        functools.partial(_mm_body, n_chunks=1, has_bias=False, has_scale=True, act=None),
        out_shape=jax.ShapeDtypeStruct((m, n_g * gd), BF16),
        grid=(n_g, m // tm),
        in_specs=[pl.BlockSpec((tm, gd), lambda g, i: (i, g)),
                  pl.BlockSpec((None, None, gd, gd), lambda g, i: (layer, g, 0, 0)),
                  pl.BlockSpec((1, gd), lambda g, i: (0, g))],
        out_specs=pl.BlockSpec((tm, gd), lambda g, i: (i, g)),
        scratch_shapes=[pltpu.VMEM((1, gd, gd), BF16)],
        compiler_params=_cparams(("arbitrary", "arbitrary")),
        name="pool_proj",
    )(x, w, scale)


def _pack_bf16_pair(y):
    half = y.shape[1] // 2
    hi = lax.bitcast_convert_type(y[:, :half].astype(BF16).astype(F32), U32)
    lo = lax.bitcast_convert_type(y[:, half:].astype(BF16).astype(F32), U32)
    return (hi & jnp.uint32(0xFFFF0000)) | (lo >> 16)


def _unpack_bf16_pair(p):
    hi = lax.bitcast_convert_type(p & jnp.uint32(0xFFFF0000), F32).astype(BF16)
    lo = lax.bitcast_convert_type(p << 16, F32).astype(BF16)
    return hi, lo


def _ln_router_body(x_ref, mix_ref, g_ref, b_ref, wr_ref, xo_ref, xp_ref, rt_ref, *, alpha, n_groups, per_group):
    y = _layer_norm(alpha * x_ref[...] + mix_ref[...], g_ref[...], b_ref[...])
    xo_ref[...] = y
    xp_ref[...] = _pack_bf16_pair(y)
    logits = jnp.dot(y, wr_ref[...], preferred_element_type=F32, precision=lax.Precision.HIGHEST)
    rows = logits.shape[0]
    lane = lax.broadcasted_iota(I32, (rows, LANE), 1).astype(F32)
    neg = -jnp.inf
    far = float(LANE)
    is_g = lane < n_groups
    gl = jnp.where(is_g, logits, neg)
    gmax = jnp.max(gl, axis=1, keepdims=True)
    gsel = jnp.min(jnp.where(gl == gmax, lane, far), axis=1, keepdims=True)
    gden = jnp.sum(jnp.where(is_g, jnp.exp(jnp.where(is_g, logits, gmax) - gmax), 0.0), axis=1, keepdims=True)
    gw = 1.0 / gden
    lo = n_groups + gsel * per_group
    el = jnp.where((lane >= lo) & (lane < lo + per_group), logits, neg)
    e1 = jnp.max(el, axis=1, keepdims=True)
    i1 = jnp.min(jnp.where(el == e1, lane, far), axis=1, keepdims=True)
    el2 = jnp.where(lane == i1, neg, el)
    e2 = jnp.max(el2, axis=1, keepdims=True)
    i2 = jnp.min(jnp.where(el2 == e2, lane, far), axis=1, keepdims=True)
    t = jnp.exp(e2 - e1)
    w1 = gw / (1.0 + t)
    w2 = gw * t / (1.0 + t)
    out = jnp.where(lane == 0, i1 - n_groups,
                    jnp.where(lane == 1, i2 - n_groups,
                              jnp.where(lane == 2, w1, jnp.where(lane == 3, w2, 0.0))))
    rt_ref[...] = out


def _ln_router(x, mix, g, b, w_router, alpha, n_groups, per_group, tm):
    t, d = x.shape
    row = lambda i: (i, 0)
    fixed = lambda i: (0, 0)
    return pl.pallas_call(
        functools.partial(_ln_router_body, alpha=alpha, n_groups=n_groups, per_group=per_group),
        out_shape=(jax.ShapeDtypeStruct((t, d), F32), jax.ShapeDtypeStruct((t, d // 2), U32),
                   jax.ShapeDtypeStruct((t, LANE), F32)),
        grid=(t // tm,),
        in_specs=[pl.BlockSpec((tm, d), row), pl.BlockSpec((tm, d), row), pl.BlockSpec((1, d), fixed),
                  pl.BlockSpec((1, d), fixed), pl.BlockSpec((d, LANE), fixed)],
        out_specs=(pl.BlockSpec((tm, d), row), pl.BlockSpec((tm, d // 2), row), pl.BlockSpec((tm, LANE), row)),
        compiler_params=_cparams(("arbitrary",)),
        name="ln_router",
    )(x, mix, g, b, w_router)


def _row_copy(src_hbm, src_row, dst, dst_row, sem):
    return pltpu.make_async_copy(src_hbm.at[pl.ds(src_row, 1)], dst.at[pl.ds(dst_row, 1)], sem)


def _dispatch_body(dest_ref, xp_hbm, init_hbm, xs_hbm, sem, *, tb):
    del init_hbm
    base = pl.program_id(0) * tb

    def copies(r, k):
        return _row_copy(xp_hbm, base + r, xs_hbm, dest_ref[(base + r) * MOE_TOPK + k], sem)

    def start(r, carry):
        for k in range(MOE_TOPK):
            copies(r, k).start()
        return carry

    def wait(r, carry):
        for k in range(MOE_TOPK):
            copies(r, k).wait()
        return carry

    lax.fori_loop(0, tb, start, 0)
    lax.fori_loop(0, tb, wait, 0)


def _dispatch(dest, xp, n_rows, tb):
    t, dh = xp.shape
    init = jnp.zeros((n_rows, dh), U32)
    return pl.pallas_call(
        functools.partial(_dispatch_body, tb=tb),
        out_shape=jax.ShapeDtypeStruct((n_rows, dh), U32),
        grid_spec=pltpu.PrefetchScalarGridSpec(
            num_scalar_prefetch=1,
            grid=(t // tb,),
            in_specs=[pl.BlockSpec(memory_space=pl.ANY), pl.BlockSpec(memory_space=pl.ANY)],
            out_specs=pl.BlockSpec(memory_space=pl.ANY),
            scratch_shapes=[pltpu.SemaphoreType.DMA],
        ),
        input_output_aliases={2: 0},
        compiler_params=_cparams(("arbitrary",)),
        name="moe_dispatch",
    )(dest, xp, init)


def _expert_changed(blk_e_ref, blk):
    prev = blk_e_ref[jnp.maximum(blk - 1, 0)]
    return (blk == 0) | (blk_e_ref[blk] != prev)


def _expert_up_body(blk_e_ref, n_used_ref, xs_ref, wg_ref, wu_ref, h_ref, wgb, wub):
    blk = pl.program_id(1)

    @pl.when(_expert_changed(blk_e_ref, blk))
    def _():
        wgb[...] = wg_ref[...].astype(BF16)
        wub[...] = wu_ref[...].astype(BF16)

    @pl.when(blk < n_used_ref[0])
    def _():
        hi, lo = _unpack_bf16_pair(xs_ref[...])
        half = hi.shape[1]
        gate = (jnp.dot(hi, wgb[:half], preferred_element_type=F32)
                + jnp.dot(lo, wgb[half:], preferred_element_type=F32))
        up = (jnp.dot(hi, wub[:half], preferred_element_type=F32)
              + jnp.dot(lo, wub[half:], preferred_element_type=F32))
        h_ref[...] = (_silu(gate) * up).astype(BF16)

    @pl.when(blk >= n_used_ref[0])
    def _():
        h_ref[...] = jnp.zeros_like(h_ref)


def _expert_up(blk_e, n_used, xs, w_gate, w_up, layer, hc):
    n_rows, dh = xs.shape
    d, hdim = w_gate.shape[2], w_gate.shape[3]
    n_blk = n_rows // MOE_BLOCK
    wspec = pl.BlockSpec((None, None, d, hc), lambda p, b, be, nu: (layer, be[b], 0, p))
    return pl.pallas_call(
        _expert_up_body,
        out_shape=jax.ShapeDtypeStruct((n_rows, hdim), BF16),
        grid_spec=pltpu.PrefetchScalarGridSpec(
            num_scalar_prefetch=2,
            grid=(hdim // hc, n_blk),
            in_specs=[pl.BlockSpec((MOE_BLOCK, dh), lambda p, b, be, nu: (b, 0)), wspec, wspec],
            out_specs=pl.BlockSpec((MOE_BLOCK, hc), lambda p, b, be, nu: (b, p)),
            scratch_shapes=[pltpu.VMEM((d, hc), BF16), pltpu.VMEM((d, hc), BF16)],
        ),
        compiler_params=_cparams(("arbitrary", "arbitrary")),
        name="expert_up",
    )(blk_e, n_used, xs, w_gate, w_up)


def _expert_down_body(blk_e_ref, n_used_ref, h_ref, wd_ref, y_ref, wdb):
    blk = pl.program_id(1)

    @pl.when(_expert_changed(blk_e_ref, blk))
    def _():
        wdb[...] = wd_ref[...].astype(BF16)

    @pl.when(blk < n_used_ref[0])
    def _():
        y_ref[...] = jnp.dot(h_ref[...], wdb[...], preferred_element_type=F32)

    @pl.when(blk >= n_used_ref[0])
    def _():
        y_ref[...] = jnp.zeros_like(y_ref)


def _expert_down(blk_e, n_used, h, w_down, layer, oc):
    n_rows, hdim = h.shape
    d = w_down.shape[3]
    n_blk = n_rows // MOE_BLOCK
    return pl.pallas_call(
        _expert_down_body,
        out_shape=jax.ShapeDtypeStruct((n_rows, d), F32),
        grid_spec=pltpu.PrefetchScalarGridSpec(
            num_scalar_prefetch=2,
            grid=(d // oc, n_blk),
            in_specs=[pl.BlockSpec((MOE_BLOCK, hdim), lambda q, b, be, nu: (b, 0)),
                      pl.BlockSpec((None, None, hdim, oc), lambda q, b, be, nu: (layer, be[b], 0, q))],
            out_specs=pl.BlockSpec((MOE_BLOCK, oc), lambda q, b, be, nu: (b, q)),
            scratch_shapes=[pltpu.VMEM((hdim, oc), BF16)],
        ),
        compiler_params=_cparams(("arbitrary", "arbitrary")),
        name="expert_down",
    )(blk_e, n_used, h, w_down)


def _combine_ln_body(dest_ref, x_ref, rt_ref, g_ref, b_ref, y_hbm, xo_ref, xb_ref, rows, sem, *, alpha, tm):
    base = pl.program_id(0) * tm

    def copies(r, k):
        return _row_copy(y_hbm, dest_ref[(base + r) * MOE_TOPK + k], rows.at[k], r, sem)

    def start(r, carry):
        for k in range(MOE_TOPK):
            copies(r, k).start()
        return carry

    def wait(r, carry):
        for k in range(MOE_TOPK):
            copies(r, k).wait()
        return carry

    lax.fori_loop(0, tm, start, 0)
    lax.fori_loop(0, tm, wait, 0)
    rt = rt_ref[...]
    ffn = rt[:, 2:3] * rows[0] + rt[:, 3:4] * rows[1]
    y = _layer_norm(alpha * x_ref[...] + ffn, g_ref[...], b_ref[...])
    xo_ref[...] = y
    xb_ref[...] = y.astype(BF16)


def _combine_ln(dest, x, rt, g, b, y_rows, alpha, tm):
    t, d = x.shape
    row = lambda i, de: (i, 0)
    fixed = lambda i, de: (0, 0)
    return pl.pallas_call(
        functools.partial(_combine_ln_body, alpha=alpha, tm=tm),
        out_shape=(jax.ShapeDtypeStruct((t, d), F32), jax.ShapeDtypeStruct((t, d), BF16)),
        grid_spec=pltpu.PrefetchScalarGridSpec(
            num_scalar_prefetch=1,
            grid=(t // tm,),
            in_specs=[pl.BlockSpec((tm, d), row), pl.BlockSpec((tm, LANE), row), pl.BlockSpec((1, d), fixed),
                      pl.BlockSpec((1, d), fixed), pl.BlockSpec(memory_space=pl.ANY)],
            out_specs=(pl.BlockSpec((tm, d), row), pl.BlockSpec((tm, d), row)),
            scratch_shapes=[pltpu.VMEM((MOE_TOPK, tm, d), F32), pltpu.SemaphoreType.DMA],
        ),
        compiler_params=_cparams(("arbitrary",)),
        name="moe_combine_ln",
    )(dest, x, rt, g, b, y_rows)


def _moe_plan(rt, n_experts, n_blocks):
    e = rt[:, :MOE_TOPK].astype(I32).reshape(-1)
    onehot = (e[:, None] == jnp.arange(n_experts, dtype=I32)[None, :]).astype(I32)
    csum = jnp.cumsum(onehot, axis=0)
    rank = jnp.sum(csum * onehot, axis=1) - 1
    counts = csum[-1]
    padded = (counts + MOE_BLOCK - 1) // MOE_BLOCK * MOE_BLOCK
    pad_end = jnp.cumsum(padded)
    pad_start = pad_end - padded
    dest = (jnp.sum(onehot * pad_start[None, :], axis=1) + rank).astype(I32)
    blk_e = jnp.minimum(jnp.searchsorted(pad_end, jnp.arange(n_blocks, dtype=I32) * MOE_BLOCK, side="right"),
                        n_experts - 1).astype(I32)
    n_used = (pad_end[-1:] // MOE_BLOCK).astype(I32)
    return dest, blk_e, n_used


def _pool_body(u_ref, st_ref, alias_ref, o_ref, ext, *, q, nb, start_pos, gd):
    del alias_ref
    c = pl.program_id(1)
    head = POOL_BUF + 1

    @pl.when(c == 0)
    def _():
        for i in range(nb):
            ext[i, 1:head] = st_ref[i]

    pos = (start_pos + c * q + lax.broadcasted_iota(I32, (q, 1), 0) + 1).astype(F32)
    for i in range(nb):
        ext[i, head:head + q] = u_ref[i * q:(i + 1) * q]
    for i in range(nb):
        for gi, w in enumerate(POOL_WINDOWS):
            sl = slice(gi * gd, (gi + 1) * gd)
            acc = ext[i, head:head + q, sl]
            cur = acc
            for k in range(1, w):
                acc = acc + ext[i, head - k:head - k + q, sl]
            pooled = acc / jnp.minimum(pos, float(w))
            o_ref[i * q:(i + 1) * q, sl] = (pooled - cur).astype(o_ref.dtype)
    for i in range(nb):
        ext[i, 0:head] = ext[i, q:q + head]


def _pool_mix(u, state, lead, out_buf, row0, n_tiles_outer, nc, q, nb, start_pos):
    t, cdim = u.shape
    gd = cdim // len(POOL_WINDOWS)
    rb0 = row0 // TILE
    nl = len(lead)
    rows = lambda b, c: (rb0 + b * nc + c, 0)
    return pl.pallas_call(
        functools.partial(_pool_body, q=q, nb=nb, start_pos=start_pos, gd=gd),
        out_shape=jax.ShapeDtypeStruct(out_buf.shape, out_buf.dtype),
        grid=(n_tiles_outer, nc),
        in_specs=[pl.BlockSpec((TILE, cdim), rows),
                  pl.BlockSpec((None,) * nl + (nb, POOL_BUF, cdim), lambda b, c: tuple(lead) + (b, 0, 0)),
                  pl.BlockSpec(memory_space=pl.ANY)],
        out_specs=pl.BlockSpec((TILE, cdim), rows),
        scratch_shapes=[pltpu.VMEM((nb, POOL_BUF + 1 + q, cdim), F32)],
        input_output_aliases={2: 0},
        compiler_params=_cparams(("arbitrary", "arbitrary")),
        name="pool_mix",
    )(u, state, out_buf)


def _ssd_body(xs_ref, bm_ref, cm_ref, z_ref, dt_ref, sx_ref, sb_ref, sc_ref, h0_ref,
              wx_ref, wb_ref, wc_ref, bx_ref, bb_ref, bc_ref, dtb_ref, alog_ref, dexp_ref, nw_ref, alias_ref,
              yb_ref, hout_ref, hs, ex, eb, ec, *, q, nb, hpg, kconv):
    del alias_ref
    c = pl.program_id(2)
    p_dim = SSM_HEAD_DIM
    off = SUBLANE - (kconv - 1)

    @pl.when(c == 0)
    def _():
        hs[...] = h0_ref[...]
        for i in range(nb):
            ex[i, off:SUBLANE] = sx_ref[i]
            eb[i, off:SUBLANE] = sb_ref[i]
            ec[i, off:SUBLANE] = sc_ref[i]

    def conv(raw_ref, e_ref, w_ref, bias_ref):
        outs = []
        for i in range(nb):
            e_ref[i, SUBLANE:SUBLANE + q] = raw_ref[i * q:(i + 1) * q]
            acc = bias_ref[...] + e_ref[i, off:off + q] * w_ref[0:1, :]
            for k in range(1, kconv):
                acc = acc + e_ref[i, off + k:off + k + q] * w_ref[k:k + 1, :]
            outs.append(acc)
            e_ref[i, 0:SUBLANE] = e_ref[i, q:q + SUBLANE]
        o = outs[0] if nb == 1 else jnp.concatenate(outs, axis=0)
        return _silu(o)

    xs = conv(xs_ref, ex, wx_ref, bx_ref)
    bm = conv(bm_ref, eb, wb_ref, bb_ref)
    cm = conv(cm_ref, ec, wc_ref, bc_ref)
    dt = _softplus(dt_ref[...] + dtb_ref[...])
    dta = dt * (-jnp.exp(alog_ref[...]))
    row = lax.broadcasted_iota(I32, (TILE, TILE), 0)
    col = lax.broadcasted_iota(I32, (TILE, TILE), 1)
    same = (row // q) == (col // q)
    causal = same & (col <= row)
    hp = lax.Precision.HIGHEST
    acs = jnp.dot(causal.astype(F32), dta, preferred_element_type=F32, precision=hp)
    tot = jnp.dot(same.astype(F32), dta, preferred_element_type=F32, precision=hp)
    acs_t = acs.T
    tot_t = tot.T
    e_acs = jnp.exp(acs)
    to_end = jnp.exp(tot - acs)
    bm_b = bm.astype(BF16)
    cm_b = cm.astype(BF16)
    scores = lax.dot_general(cm_b, bm_b, _NT, preferred_element_type=F32)
    lo_half = lax.broadcasted_iota(I32, (TILE, LANE), 1) < p_dim

    y_parts, xw_t_parts, ea_parts = [], [], []
    for p in range(hpg // 2):
        j0, j1 = 2 * p, 2 * p + 1

        def expand(v):
            return jnp.where(lo_half, v[:, j0:j0 + 1], v[:, j1:j1 + 1])

        xs_p = xs[:, p * LANE:(p + 1) * LANE]
        xdt = xs_p * expand(dt)
        xdt_b = xdt.astype(BF16)
        ys = []
        for j in (j0, j1):
            decay = jnp.exp(jnp.where(causal, acs[:, j:j + 1] - acs_t[j:j + 1, :], -jnp.inf))
            ys.append(jnp.dot((scores * decay).astype(BF16), xdt_b, preferred_element_type=F32))
        y_parts.append(jnp.where(lo_half, ys[0], ys[1]))
        xw_t_parts.append((xdt * expand(to_end)).T.astype(BF16))
        ea_parts.append(expand(e_acs))
    y_diag = jnp.concatenate(y_parts, axis=1)
    xw_t = jnp.concatenate(xw_t_parts, axis=0)
    ea = jnp.concatenate(ea_parts, axis=1)

    sub = lax.broadcasted_iota(I32, (TILE, 1), 0) // q
    y_off = None
    for i in range(nb):
        h_i = hs[i]
        full = lax.dot_general(cm_b, h_i.astype(BF16), _NT, preferred_element_type=F32)
        y_off = full if nb == 1 else (jnp.where(sub == i, full, 0.0) + (0.0 if y_off is None else y_off))
        bm_i = bm_b if nb == 1 else jnp.where(sub == i, bm, 0.0).astype(BF16)
        st = jnp.dot(xw_t, bm_i, preferred_element_type=F32)
        for j in range(hpg):
            rs = slice(j * p_dim, (j + 1) * p_dim)
            cd = jnp.exp(tot_t[j:j + 1, i * q:i * q + 1])
            hs[i, rs, :] = h_i[rs, :] * cd + st[rs, :]

    y = y_diag + y_off * ea + xs * dexp_ref[...]
    y = y * _silu(z_ref[...])
    ms = jnp.mean(y * y, axis=1, keepdims=True)
    yb_ref[...] = (y * lax.rsqrt(ms + RMS_EPS) * nw_ref[...]).astype(yb_ref.dtype)

    @pl.when(c == pl.num_programs(2) - 1)
    def _():
        hout_ref[...] = hs[...]


def _ssd_mix(xbc, z, dtr, conv_state, conv_lead, h0, h0_lead, conv_w, conv_b, dtb, alog, dexp, nw, layer,
             out_buf, row0, n_outer, nc, q, nb, inner):
    n_state = LANE
    gw = inner // SSM_GROUPS
    hpg = gw // SSM_HEAD_DIM
    assert hpg % 2 == 0 and gw % LANE == 0
    kconv = conv_w.shape[1]
    rb0 = row0 // TILE
    b_blk = inner // n_state
    c_blk = (inner + SSM_GROUPS * n_state) // n_state
    ncl, nhl = len(conv_lead), len(h0_lead)

    def rows(colf):
        return lambda b, g, c: (rb0 + b * nc + c, colf(g))

    def cstate(width, colf):
        return pl.BlockSpec((None,) * ncl + (nb, kconv - 1, width), lambda b, g, c: tuple(conv_lead) + (b, 0, colf(g)))

    def wspec(nrows, width, colf):
        return pl.BlockSpec((None, nrows, width), lambda b, g, c: (layer, 0, colf(g)))

    xcol = lambda g: g
    bcol = lambda g: b_blk + g
    ccol = lambda g: c_blk + g
    in_specs = [
        pl.BlockSpec((TILE, gw), rows(xcol)), pl.BlockSpec((TILE, n_state), rows(bcol)),
        pl.BlockSpec((TILE, n_state), rows(ccol)), pl.BlockSpec((TILE, gw), rows(xcol)),
        pl.BlockSpec((TILE, LANE), rows(xcol)),
        cstate(gw, xcol), cstate(n_state, bcol), cstate(n_state, ccol),
        pl.BlockSpec((None,) * nhl + (nb, gw, n_state), lambda b, g, c: tuple(h0_lead) + (b, g, 0)),
        wspec(kconv, gw, xcol), wspec(kconv, n_state, bcol), wspec(kconv, n_state, ccol),
        wspec(1, gw, xcol), wspec(1, n_state, bcol), wspec(1, n_state, ccol),
        wspec(1, LANE, xcol), wspec(1, LANE, xcol), wspec(1, gw, xcol), wspec(1, gw, xcol),
        pl.BlockSpec(memory_space=pl.ANY),
    ]
    n_seq = n_outer * nb
    return pl.pallas_call(
        functools.partial(_ssd_body, q=q, nb=nb, hpg=hpg, kconv=kconv),
        out_shape=(jax.ShapeDtypeStruct(out_buf.shape, out_buf.dtype),
                   jax.ShapeDtypeStruct((n_seq, inner, n_state), F32)),
        grid=(n_outer, SSM_GROUPS, nc),
        in_specs=in_specs,
        out_specs=(pl.BlockSpec((TILE, gw), rows(xcol)),
                   pl.BlockSpec((nb, gw, n_state), lambda b, g, c: (b, g, 0))),
        scratch_shapes=[pltpu.VMEM((nb, gw, n_state), F32), pltpu.VMEM((nb, SUBLANE + q, gw), F32),
                        pltpu.VMEM((nb, SUBLANE + q, n_state), F32), pltpu.VMEM((nb, SUBLANE + q, n_state), F32)],
        input_output_aliases={19: 0},
        compiler_params=_cparams(("arbitrary", "arbitrary", "arbitrary")),
        name="ssd_mix",
    )(xbc, xbc, xbc, z, dtr, conv_state, conv_state, conv_state, h0, conv_w, conv_w, conv_w, conv_b, conv_b, conv_b,
      dtb, alog, dexp, nw, out_buf)


def _cmlp_body(u_ref, v_ref, ws_ref, bs_ref, g_ref, b_ref, alias_ref, *out_refs, q, n_heads, emit_v):
    del alias_ref
    o_ref = out_refs[0]
    vn = _layer_norm(v_ref[...], g_ref[...], b_ref[...])
    if emit_v:
        out_refs[1][...] = vn
    vb = vn.astype(BF16)
    row = lax.broadcasted_iota(I32, (TILE, TILE), 0)
    col = lax.broadcasted_iota(I32, (TILE, TILE), 1)
    causal = ((row // q) == (col // q)) & (col <= row)
    hd = vb.shape[1] // n_heads
    bs = bs_ref[...]
    for h in range(n_heads):
        sl = slice(h * hd, (h + 1) * hd)
        wm = jnp.where(causal, ws_ref[h], 0.0).astype(BF16)
        mixed = jnp.dot(wm, vb[:, sl], preferred_element_type=F32) + bs[:, h:h + 1]
        o_ref[:, sl] = (u_ref[:, sl].astype(F32) * mixed).astype(o_ref.dtype)


def _cmlp_mix(u, v, ws, ws_lead, bs_t, vg, vb, out_buf, row0, n_tiles, q, emit_v):
    t, cdim = u.shape
    n_heads = ws.shape[-3]
    rb0 = row0 // TILE
    nl = len(ws_lead)
    rows = lambda i: (rb0 + i, 0)
    fixed = lambda i: (0, 0)
    out_shape = [jax.ShapeDtypeStruct(out_buf.shape, out_buf.dtype)]
    out_specs = [pl.BlockSpec((TILE, cdim), rows)]
    if emit_v:
        out_shape.append(jax.ShapeDtypeStruct((n_tiles * TILE, cdim), F32))
        out_specs.append(pl.BlockSpec((TILE, cdim), lambda i: (i, 0)))
    return pl.pallas_call(
        functools.partial(_cmlp_body, q=q, n_heads=n_heads, emit_v=emit_v),
        out_shape=tuple(out_shape),
        grid=(n_tiles,),
        in_specs=[pl.BlockSpec((TILE, cdim), rows), pl.BlockSpec((TILE, cdim), rows),
                  pl.BlockSpec((None,) * nl + (n_heads, TILE, TILE), lambda i: tuple(ws_lead) + (0, 0, 0)),
                  pl.BlockSpec((TILE, LANE), fixed), pl.BlockSpec((1, cdim), fixed), pl.BlockSpec((1, cdim), fixed),
                  pl.BlockSpec(memory_space=pl.ANY)],
        out_specs=tuple(out_specs),
        input_output_aliases={6: 0},
        compiler_params=_cparams(("arbitrary",)),
        name="cmlp_mix",
    )(u, v, ws, bs_t, vg, vb, out_buf)


def _pick(n, target):
    t = min(n, target)
    while n % t:
        t -= SUBLANE
    return t


def kernel(x_prompt, x_sample, state_pool, state_conv, state_ssm, w_in_even, pool_w, pool_scale, conv_w, conv_b, dt_bias, a_log, d_skip, ssm_norm_w, w_out_even, cmlp_w_in, cmlp_b_in, cmlp_v_g, cmlp_v_b, cmlp_w_s, cmlp_b_s, cmlp_w_out, ln_mix_g, ln_mix_b, ln_ffn_g, ln_ffn_b, moe_w_group, moe_w_expert, moe_w_gate, moe_w_up, moe_w_down):
    bp, lp, d = x_prompt.shape
    bs, ls, _ = x_sample.shape
    depth = ln_mix_g.shape[0]
    alpha = float((2 * depth) ** 0.25)
    tp, ts = bp * lp, bs * ls
    t = tp + ts
    pool_dim = state_pool.shape[-1]
    conv_dim = state_conv.shape[-1]
    n_heads_ssm = state_ssm.shape[2]
    inner = n_heads_ssm * SSM_HEAD_DIM
    n_state = state_ssm.shape[-1]
    hpg = n_heads_ssm // SSM_GROUPS
    n_groups = moe_w_group.shape[-1]
    n_experts = moe_w_expert.shape[-1]
    per_group = n_experts // n_groups
    cdim = cmlp_w_out.shape[1]
    c_heads = cmlp_w_s.shape[1]
    assert n_state == LANE and SEQ_CHUNK == TILE and lp % TILE == 0 and TILE % ls == 0 and ts % TILE == 0
    assert n_groups + n_experts <= LANE and hpg <= LANE
    q_s = math.gcd(ls, SEQ_CHUNK)
    assert q_s == ls
    nb_s = TILE // q_s
    n_blocks = -(-(t * MOE_TOPK) // MOE_BLOCK) + n_experts
    n_rows = n_blocks * MOE_BLOCK

    tm = _pick(t, 512)
    tm_ln = _pick(t, 256)
    tn = 512

    x = jnp.concatenate([x_prompt.reshape(tp, d), x_sample.reshape(ts, d)], axis=0)
    xb = x.astype(BF16)

    dt_col = pool_dim + inner + conv_dim

    def group_pad(v):
        lead = v.shape[:-1]
        v = v.reshape(lead + (SSM_GROUPS, hpg))
        v = jnp.pad(v, [(0, 0)] * len(lead) + [(0, 0), (0, LANE - hpg)])
        return v.reshape(lead + (SSM_GROUPS * LANE,))

    w_dt = group_pad(w_in_even[:, :, dt_col:])
    dtb_p = group_pad(dt_bias)[:, None, :]
    alog_p = group_pad(a_log)[:, None, :]
    dexp = jnp.repeat(d_skip, SSM_HEAD_DIM, axis=-1)[:, None, :]
    nw3 = ssm_norm_w[:, None, :]
    conv_b3 = conv_b[:, None, :]
    w_router = jnp.concatenate(
        [moe_w_group, moe_w_expert, jnp.zeros((depth, d, LANE - n_groups - n_experts), F32)], axis=-1)
    zero_pool = jnp.zeros((bp, POOL_BUF, pool_dim), F32)
    zero_conv = jnp.zeros((bp, state_conv.shape[2], conv_dim), F32)
    zero_ssm = jnp.zeros((bp, inner, n_state), F32)
    ssm_in = state_ssm.reshape(state_ssm.shape[0], bs, inner, n_state)
    eye = jnp.eye(nb_s, dtype=F32)
    ws_s = jnp.einsum("ab,lhts->lhatbs", eye, cmlp_w_s[:, :, :q_s, :q_s]).reshape(-1, c_heads, TILE, TILE)
    bs_p = jnp.pad(jnp.swapaxes(cmlp_b_s, 1, 2), ((0, 0), (0, 0), (0, LANE - c_heads)))
    bs_s = jnp.tile(bs_p[:, :q_s], (1, nb_s, 1))

    pool_out, conv_out, ssm_out, v_out = [], [], [], []
    for layer in range(depth):
        i = layer // 2
        if layer % 2 == 0:
            lhs = [(xb, 0)]
            u_a = _matmul(lhs, w_in_even, (i,), (0,), d, 0, pool_dim, tn, tm, F32, name="in_pool")
            z = _matmul(lhs, w_in_even, (i,), (0,), d, pool_dim, inner, tn, tm, F32, name="in_gate")
            xbc = _matmul(lhs, w_in_even, (i,), (0,), d, pool_dim + inner, conv_dim, tn, tm, F32, name="in_conv")
            dtr = _matmul(lhs, w_dt, (i,), (0,), d, 0, SSM_GROUPS * LANE, tn, tm, F32, name="in_dt")

            mixed = jnp.zeros((t, pool_dim), BF16)
            mixed = _pool_mix(u_a, zero_pool, (), mixed, 0, bp, lp // TILE, TILE, 1, 0)
            mixed = _pool_mix(u_a, state_pool, (i,), mixed, tp, ts // TILE, 1, q_s, nb_s, PAST_LEN)
            y_a = _group_matmul(mixed, pool_w, i, pool_scale[i][None, :], tm)

            y_b = jnp.zeros((t, inner), BF16)
            y_b, ssm_p = _ssd_mix(xbc, z, dtr, zero_conv, (), zero_ssm, (), conv_w, conv_b3, dtb_p, alog_p, dexp, nw3,
                                  i, y_b, 0, bp, lp // TILE, TILE, 1, inner)
            y_b, ssm_s = _ssd_mix(xbc, z, dtr, state_conv, (i,), ssm_in, (i,), conv_w, conv_b3, dtb_p, alog_p, dexp,
                                  nw3, i, y_b, tp, ts // TILE, 1, q_s, nb_s, inner)

            kc = pool_dim
            assert inner % kc == 0
            lhs_out = [(y_a, 0)] + [(y_b, k) for k in range(inner // kc)]
            mix = _matmul(lhs_out, w_out_even, (i,), tuple(range(1 + inner // kc)), kc, 0, d, tn, _pick(t, 256), F32,
                          name="out_even")

            u_p = u_a[:tp].reshape(bp, lp, pool_dim)
            pool_out.append((u_p[:, lp - POOL_BUF:], jnp.concatenate(
                [state_pool[i], u_a[tp:].reshape(bs, ls, pool_dim)], axis=1)[:, -POOL_BUF:]))
            kc1 = state_conv.shape[2]
            conv_out.append((xbc[:tp].reshape(bp, lp, conv_dim)[:, lp - kc1:], jnp.concatenate(
                [state_conv[i], xbc[tp:].reshape(bs, ls, conv_dim)], axis=1)[:, -kc1:]))
            ssm_out.append((ssm_p.reshape(bp, n_heads_ssm, SSM_HEAD_DIM, n_state),
                            ssm_s.reshape(bs, n_heads_ssm, SSM_HEAD_DIM, n_state)))
        else:
            lhs = [(xb, 0)]
            bias = cmlp_b_in[i][None, :]
            u = _matmul(lhs, cmlp_w_in, (i,), (0,), d, 0, cdim, tn, tm, BF16, bias=bias[:, :cdim], act="gelu",
                        name="cmlp_in_u")
            v = _matmul(lhs, cmlp_w_in, (i,), (0,), d, cdim, cdim, tn, tm, F32, bias=bias[:, cdim:], act="gelu",
                        name="cmlp_in_v")
            vg, vb_ = cmlp_v_g[i][None, :], cmlp_v_b[i][None, :]
            gated = jnp.zeros((t, cdim), BF16)
            (gated,) = _cmlp_mix(u, v, cmlp_w_s, (i,), bs_p[i], vg, vb_, gated, 0, tp // TILE, SEQ_CHUNK, False)
            gated, v_n = _cmlp_mix(u, v, ws_s, (i,), bs_s[i], vg, vb_, gated, tp, ts // TILE, q_s, True)
            v_out.append(v_n.reshape(bs, ls, cdim))
            mix = _matmul([(gated, 0)], cmlp_w_out, (i,), (0,), cdim, 0, d, tn, tm, F32, name="cmlp_out")

        x, xp, rt = _ln_router(x, mix, ln_mix_g[layer][None, :], ln_mix_b[layer][None, :], w_router[layer], alpha,
                               n_groups, per_group, tm_ln)
        dest, blk_e, n_used = _moe_plan(rt, n_experts, n_blocks)
        xs = _dispatch(dest, xp, n_rows, tm_ln)
        hdim = moe_w_gate.shape[-1]
        h = _expert_up(blk_e, n_used, xs, moe_w_gate, moe_w_up, layer, min(hdim, 512))
        y_rows = _expert_down(blk_e, n_used, h, moe_w_down, layer, min(d, 2048))
        x, xb = _combine_ln(dest, x, rt, ln_ffn_g[layer][None, :], ln_ffn_b[layer][None, :], y_rows, alpha, tm_ln)

    y_prompt = x[:tp].reshape(bp, lp, d)
    y_sample = x[tp:].reshape(bs, ls, d)
    pool_p = jnp.stack([a for a, _ in pool_out])
    pool_s = jnp.stack([b for _, b in pool_out])
    conv_p = jnp.stack([a for a, _ in conv_out])
    conv_s = jnp.stack([b for _, b in conv_out])
    ssm_p = jnp.stack([a for a, _ in ssm_out])
    ssm_s = jnp.stack([b for _, b in ssm_out])
    v_s = jnp.stack(v_out)
    return (y_prompt, y_sample, pool_p, pool_s, conv_p, conv_s, ssm_p, ssm_s, v_s)
```

```python
import functools
import math

import jax
import jax.numpy as jnp
from jax import lax
from jax.experimental import pallas as pl
from jax.experimental.pallas import tpu as pltpu

F32 = jnp.float32
BF16 = jnp.bfloat16
U32 = jnp.uint32
I32 = jnp.int32

POOL_WINDOWS = (2, 4, 8, 16)
POOL_BUF = max(POOL_WINDOWS) - 1
SSM_HEAD_DIM = 64
SSM_GROUPS = 8
SEQ_CHUNK = 128
MOE_TOPK = 2
PAST_LEN = 16384
LN_EPS = 1e-5
RMS_EPS = 1e-5

LANE = 128
SUBLANE = 8
V7X_VMEM_BYTES = 64 * 1024 * 1024
VMEM_LIMIT = V7X_VMEM_BYTES - 8 * 1024 * 1024
TILE = 128
MOE_BLOCK = 256

_NT = (((1,), (1,)), ((), ()))


def _cparams(sem):
    return pltpu.CompilerParams(dimension_semantics=sem, vmem_limit_bytes=VMEM_LIMIT)


def _sigmoid(x):
    return 0.5 * (jnp.tanh(0.5 * x) + 1.0)


def _silu(x):
    return x * _sigmoid(x)


def _gelu_tanh(x):
    return 0.5 * x * (1.0 + jnp.tanh(math.sqrt(2.0 / math.pi) * (x + 0.044715 * (x * x * x))))


def _softplus(x):
    return jnp.maximum(x, 0.0) + jnp.log1p(jnp.exp(-jnp.abs(x)))


def _layer_norm(y, g, b):
    mu = jnp.mean(y, axis=-1, keepdims=True)
    d = y - mu
    var = jnp.mean(d * d, axis=-1, keepdims=True)
    return d * lax.rsqrt(var + LN_EPS) * g + b


def _mm_body(*refs, n_chunks, has_bias, has_scale, act):
    xs = refs[:n_chunks]
    ws = refs[n_chunks:2 * n_chunks]
    pos = 2 * n_chunks
    bias_ref = scale_ref = None
    if has_bias:
        bias_ref = refs[pos]
        pos += 1
    if has_scale:
        scale_ref = refs[pos]
        pos += 1
    o_ref, wb = refs[pos], refs[pos + 1]

    @pl.when(pl.program_id(1) == 0)
    def _():
        for c in range(n_chunks):
            wb[c] = ws[c][...].astype(BF16)

    acc = None
    for c in range(n_chunks):
        d = jnp.dot(xs[c][...], wb[c], preferred_element_type=F32)
        acc = d if acc is None else acc + d
    if has_scale:
        acc = acc * scale_ref[...]
    if has_bias:
        acc = acc + bias_ref[...]
    if act == "gelu":
        acc = _gelu_tanh(acc)
    o_ref[...] = acc.astype(o_ref.dtype)


def _matmul(lhs, w, w_lead, w_row_blocks, kc, col0, n_cols, tn, tm, out_dtype, bias=None, act=None, name="mm"):
    n_chunks = len(lhs)
    m = lhs[0][0].shape[0]
    assert m % tm == 0 and n_cols % tn == 0 and col0 % tn == 0
    cb0 = col0 // tn
    nl = len(w_lead)
    in_specs = []
    for _, cb in lhs:
        in_specs.append(pl.BlockSpec((tm, kc), functools.partial(lambda j, i, cb: (i, cb), cb=cb)))
    for rb in w_row_blocks:
        in_specs.append(pl.BlockSpec((None,) * nl + (kc, tn),
                                     functools.partial(lambda j, i, rb: tuple(w_lead) + (rb, cb0 + j), rb=rb)))
    args = [a for a, _ in lhs] + [w] * n_chunks
    if bias is not None:
        in_specs.append(pl.BlockSpec((1, tn), lambda j, i: (0, j)))
        args.append(bias)
    return pl.pallas_call(
        functools.partial(_mm_body, n_chunks=n_chunks, has_bias=bias is not None, has_scale=False, act=act),
        out_shape=jax.ShapeDtypeStruct((m, n_cols), out_dtype),
        grid=(n_cols // tn, m // tm),
        in_specs=in_specs,
        out_specs=pl.BlockSpec((tm, tn), lambda j, i: (i, j)),
        scratch_shapes=[pltpu.VMEM((n_chunks, kc, tn), BF16)],
        compiler_params=_cparams(("arbitrary", "arbitrary")),
        name=name,
    )(*args)


def _group_matmul(x, w, layer, scale, tm):
    m = x.shape[0]
    _, n_g, gd, _ = w.shape
    return pl.pallas_call(
        functools.partial(_mm_body, n_chunks=1, has_bias=False, has_scale=True, act=None),
        out_shape=jax.ShapeDtypeStruct((m, n_g * gd), BF16),
        grid=(n_g, m // tm),
        in_specs=[pl.BlockSpec((tm, gd), lambda g, i: (i, g)),
                  pl.BlockSpec((None, None, gd, gd), lambda g, i: (layer, g, 0, 0)),
                  pl.BlockSpec((1, gd), lambda g, i: (0, g))],
        out_specs=pl.BlockSpec((tm, gd), lambda g, i: (i, g)),
        scratch_shapes=[pltpu.VMEM((1, gd, gd), BF16)],
        compiler_params=_cparams(("arbitrary", "arbitrary")),
        name="pool_proj",
    )(x, w, scale)


def _pack_bf16_pair(y):
    half = y.shape[1] // 2
    hi = lax.bitcast_convert_type(y[:, :half].astype(BF16).astype(F32), U32)
    lo = lax.bitcast_convert_type(y[:, half:].astype(BF16).astype(F32), U32)
    return (hi & jnp.uint32(0xFFFF0000)) | (lo >> 16)


def _unpack_bf16_pair(p):
    hi = lax.bitcast_convert_type(p & jnp.uint32(0xFFFF0000), F32).astype(BF16)
    lo = lax.bitcast_convert_type(p << 16, F32).astype(BF16)
    return hi, lo


def _ln_router_body(x_ref, mix_ref, g_ref, b_ref, wr_ref, xo_ref, xp_ref, rt_ref, *, alpha, n_groups, per_group):
    y = _layer_norm(alpha * x_ref[...] + mix_ref[...], g_ref[...], b_ref[...])
    xo_ref[...] = y
    xp_ref[...] = _pack_bf16_pair(y)
    logits = jnp.dot(y.astype(BF16), wr_ref[...], preferred_element_type=F32)
    rows = logits.shape[0]
    lane = lax.broadcasted_iota(I32, (rows, LANE), 1).astype(F32)
    neg = -jnp.inf
    far = float(LANE)
    is_g = lane < n_groups
    gl = jnp.where(is_g, logits, neg)
    gmax = jnp.max(gl, axis=1, keepdims=True)
    gsel = jnp.min(jnp.where(gl == gmax, lane, far), axis=1, keepdims=True)
    gden = jnp.sum(jnp.where(is_g, jnp.exp(jnp.where(is_g, logits, gmax) - gmax), 0.0), axis=1, keepdims=True)
    gw = 1.0 / gden
    lo = n_groups + gsel * per_group
    el = jnp.where((lane >= lo) & (lane < lo + per_group), logits, neg)
    e1 = jnp.max(el, axis=1, keepdims=True)
    i1 = jnp.min(jnp.where(el == e1, lane, far), axis=1, keepdims=True)
    el2 = jnp.where(lane == i1, neg, el)
    e2 = jnp.max(el2, axis=1, keepdims=True)
    i2 = jnp.min(jnp.where(el2 == e2, lane, far), axis=1, keepdims=True)
    t = jnp.exp(e2 - e1)
    w1 = gw / (1.0 + t)
    w2 = gw * t / (1.0 + t)
    out = jnp.where(lane == 0, i1 - n_groups,
                    jnp.where(lane == 1, i2 - n_groups,
                              jnp.where(lane == 2, w1, jnp.where(lane == 3, w2, 0.0))))
    rt_ref[...] = out


def _ln_router(x, mix, g, b, w_router, alpha, n_groups, per_group, tm):
    t, d = x.shape
    row = lambda i: (i, 0)
    fixed = lambda i: (0, 0)
    return pl.pallas_call(
        functools.partial(_ln_router_body, alpha=alpha, n_groups=n_groups, per_group=per_group),
        out_shape=(jax.ShapeDtypeStruct((t, d), F32), jax.ShapeDtypeStruct((t, d // 2), U32),
                   jax.ShapeDtypeStruct((t, LANE), F32)),
        grid=(t // tm,),
        in_specs=[pl.BlockSpec((tm, d), row), pl.BlockSpec((tm, d), row), pl.BlockSpec((1, d), fixed),
                  pl.BlockSpec((1, d), fixed), pl.BlockSpec((d, LANE), fixed)],
        out_specs=(pl.BlockSpec((tm, d), row), pl.BlockSpec((tm, d // 2), row), pl.BlockSpec((tm, LANE), row)),
        compiler_params=_cparams(("arbitrary",)),
        name="ln_router",
    )(x, mix, g, b, w_router)


def _row_copy(src_hbm, src_row, dst, dst_row, sem):
    return pltpu.make_async_copy(src_hbm.at[pl.ds(src_row, 1)], dst.at[pl.ds(dst_row, 1)], sem)


def _row_gather(n, src_hbm, src_row, dst, sem, wait):
    def body(r, carry):
        cp = _row_copy(src_hbm, src_row(r), dst, r, sem)
        if wait:
            cp.wait()
        else:
            cp.start()
        return carry

    lax.fori_loop(0, n, body, 0)


def _dispatch_body(tok_ref, meta_ref, xp_hbm, o_ref, buf, sems):
    b = pl.program_id(0)
    n_used = meta_ref[0]

    def rows(blk, wait):
        slot = blk % 2
        _row_gather(MOE_BLOCK, xp_hbm, lambda r: tok_ref[blk * MOE_BLOCK + r], buf.at[slot], sems.at[slot], wait)

    @pl.when((b == 0) & (n_used > 0))
    def _():
        rows(b, False)

    @pl.when(b + 1 < n_used)
    def _():
        rows(b + 1, False)

    @pl.when(b < n_used)
    def _():
        rows(b, True)
        o_ref[...] = buf[b % 2]

    @pl.when(b >= n_used)
    def _():
        o_ref[...] = jnp.zeros_like(o_ref)


def _dispatch(row_tok, meta, xp):
    dh = xp.shape[1]
    n_rows = row_tok.shape[0]
    return pl.pallas_call(
        _dispatch_body,
        out_shape=jax.ShapeDtypeStruct((n_rows, dh), U32),
        grid_spec=pltpu.PrefetchScalarGridSpec(
            num_scalar_prefetch=2,
            grid=(n_rows // MOE_BLOCK,),
            in_specs=[pl.BlockSpec(memory_space=pl.ANY)],
            out_specs=pl.BlockSpec((MOE_BLOCK, dh), lambda b, tok, meta: (b, 0)),
            scratch_shapes=[pltpu.VMEM((2, MOE_BLOCK, dh), U32), pltpu.SemaphoreType.DMA((2,))],
        ),
        compiler_params=_cparams(("arbitrary",)),
        name="moe_dispatch",
    )(row_tok, meta, xp)


def _weight_jobs(first_ref, rank_ref, dist_ref, meta_ref, w_hbms, layer, chunk, wbuf, sems):
    p, b = pl.program_id(0), pl.program_id(1)
    n_phase = pl.num_programs(0)
    nd = meta_ref[1]
    r = rank_ref[b]
    is_first = first_ref[b] == 1
    slot = (p * nd + r) & 1

    def copies(phase, rnk, slot_):
        e = dist_ref[rnk]
        col = pl.multiple_of(phase * chunk, chunk)
        return [pltpu.make_async_copy(w.at[layer, e, :, pl.ds(col, chunk)], wbuf.at[slot_, i], sems.at[slot_])
                for i, w in enumerate(w_hbms)]

    @pl.when(is_first & (p == 0) & (r == 0))
    def _():
        for cp in copies(p, r, slot):
            cp.start()

    wrap = r + 1 >= nd
    nxt_p = jnp.where(wrap, p + 1, p)
    nxt_r = jnp.where(wrap, 0, r + 1)

    @pl.when(is_first & (nxt_p < n_phase))
    def _():
        for cp in copies(nxt_p, nxt_r, 1 - slot):
            cp.start()

    @pl.when(is_first)
    def _():
        for cp in copies(p, r, slot):
            cp.wait()

    return is_first, slot


def _expert_up_body(first_ref, rank_ref, dist_ref, meta_ref, xs_ref, wg_hbm, wu_hbm, h_ref, wbuf, wgb, wub, sems,
                    *, layer, hc):
    is_first, slot = _weight_jobs(first_ref, rank_ref, dist_ref, meta_ref, (wg_hbm, wu_hbm), layer, hc, wbuf, sems)
    blk = pl.program_id(1)
    n_used = meta_ref[0]

    @pl.when(is_first)
    def _():
        wgb[...] = wbuf[slot, 0].astype(BF16)
        wub[...] = wbuf[slot, 1].astype(BF16)

    @pl.when(blk < n_used)
    def _():
        hi, lo = _unpack_bf16_pair(xs_ref[...])
        half = hi.shape[1]
        gate = (jnp.dot(hi, wgb[:half], preferred_element_type=F32)
                + jnp.dot(lo, wgb[half:], preferred_element_type=F32))
        up = (jnp.dot(hi, wub[:half], preferred_element_type=F32)
              + jnp.dot(lo, wub[half:], preferred_element_type=F32))
        h_ref[...] = (_silu(gate) * up).astype(BF16)

    @pl.when(blk >= n_used)
    def _():
        h_ref[...] = jnp.zeros_like(h_ref)


def _expert_up(plan, xs, w_gate, w_up, layer, hc):
    n_rows, dh = xs.shape
    d, hdim = w_gate.shape[2], w_gate.shape[3]
    n_blk = n_rows // MOE_BLOCK
    return pl.pallas_call(
        functools.partial(_expert_up_body, layer=layer, hc=hc),
        out_shape=jax.ShapeDtypeStruct((n_rows, hdim), BF16),
        grid_spec=pltpu.PrefetchScalarGridSpec(
            num_scalar_prefetch=4,
            grid=(hdim // hc, n_blk),
            in_specs=[pl.BlockSpec((MOE_BLOCK, dh), lambda p, b, *_: (b, 0)),
                      pl.BlockSpec(memory_space=pl.ANY), pl.BlockSpec(memory_space=pl.ANY)],
            out_specs=pl.BlockSpec((MOE_BLOCK, hc), lambda p, b, *_: (b, p)),
            scratch_shapes=[pltpu.VMEM((2, 2, d, hc), F32), pltpu.VMEM((d, hc), BF16), pltpu.VMEM((d, hc), BF16),
                            pltpu.SemaphoreType.DMA((2,))],
        ),
        compiler_params=_cparams(("arbitrary", "arbitrary")),
        name="expert_up",
    )(*plan, xs, w_gate, w_up)


def _expert_down_body(first_ref, rank_ref, dist_ref, meta_ref, h_ref, wd_hbm, y_ref, wbuf, wdb, sems, *, layer, oc):
    is_first, slot = _weight_jobs(first_ref, rank_ref, dist_ref, meta_ref, (wd_hbm,), layer, oc, wbuf, sems)
    blk = pl.program_id(1)
    n_used = meta_ref[0]

    @pl.when(is_first)
    def _():
        wdb[...] = wbuf[slot, 0].astype(BF16)

    @pl.when(blk < n_used)
    def _():
        y_ref[...] = jnp.dot(h_ref[...], wdb[...], preferred_element_type=F32)

    @pl.when(blk >= n_used)
    def _():
        y_ref[...] = jnp.zeros_like(y_ref)


def _expert_down(plan, h, w_down, layer, oc):
    n_rows, hdim = h.shape
    d = w_down.shape[3]
    n_blk = n_rows // MOE_BLOCK
    return pl.pallas_call(
        functools.partial(_expert_down_body, layer=layer, oc=oc),
        out_shape=jax.ShapeDtypeStruct((n_rows, d), F32),
        grid_spec=pltpu.PrefetchScalarGridSpec(
            num_scalar_prefetch=4,
            grid=(d // oc, n_blk),
            in_specs=[pl.BlockSpec((MOE_BLOCK, hdim), lambda q, b, *_: (b, 0)), pl.BlockSpec(memory_space=pl.ANY)],
            out_specs=pl.BlockSpec((MOE_BLOCK, oc), lambda q, b, *_: (b, q)),
            scratch_shapes=[pltpu.VMEM((2, 1, hdim, oc), F32), pltpu.VMEM((hdim, oc), BF16),
                            pltpu.SemaphoreType.DMA((2,))],
        ),
        compiler_params=_cparams(("arbitrary", "arbitrary")),
        name="expert_down",
    )(*plan, h, w_down)


def _combine_ln_body(dest_ref, x_ref, rt_ref, g_ref, b_ref, y_hbm, xo_ref, xb_ref, rows, sems, *, alpha, tm):
    s = pl.program_id(0)

    def gather(step, wait):
        slot = step % 2
        for k in range(MOE_TOPK):
            _row_gather(tm, y_hbm, lambda r: dest_ref[(step * tm + r) * MOE_TOPK + k], rows.at[slot, k],
                        sems.at[slot], wait)

    @pl.when(s == 0)
    def _():
        gather(s, False)

    @pl.when(s + 1 < pl.num_programs(0))
    def _():
        gather(s + 1, False)

    gather(s, True)
    slot = s % 2
    rt = rt_ref[...]
    ffn = rt[:, 2:3] * rows[slot, 0] + rt[:, 3:4] * rows[slot, 1]
    y = _layer_norm(alpha * x_ref[...] + ffn, g_ref[...], b_ref[...])
    xo_ref[...] = y
    xb_ref[...] = y.astype(BF16)


def _combine_ln(dest, x, rt, g, b, y_rows, alpha, tm):
    t, d = x.shape
    row = lambda i, de: (i, 0)
    fixed = lambda i, de: (0, 0)
    return pl.pallas_call(
        functools.partial(_combine_ln_body, alpha=alpha, tm=tm),
        out_shape=(jax.ShapeDtypeStruct((t, d), F32), jax.ShapeDtypeStruct((t, d), BF16)),
        grid_spec=pltpu.PrefetchScalarGridSpec(
            num_scalar_prefetch=1,
            grid=(t // tm,),
            in_specs=[pl.BlockSpec((tm, d), row), pl.BlockSpec((tm, LANE), row), pl.BlockSpec((1, d), fixed),
                      pl.BlockSpec((1, d), fixed), pl.BlockSpec(memory_space=pl.ANY)],
            out_specs=(pl.BlockSpec((tm, d), row), pl.BlockSpec((tm, d), row)),
            scratch_shapes=[pltpu.VMEM((2, MOE_TOPK, tm, d), F32), pltpu.SemaphoreType.DMA((2,))],
        ),
        compiler_params=_cparams(("arbitrary",)),
        name="moe_combine_ln",
    )(dest, x, rt, g, b, y_rows)


def _moe_plan(rt, n_experts, n_blocks):
    n_assign = rt.shape[0] * MOE_TOPK
    e = rt[:, :MOE_TOPK].astype(I32).reshape(-1)
    onehot = (e[:, None] == jnp.arange(n_experts, dtype=I32)[None, :]).astype(I32)
    csum = jnp.cumsum(onehot, axis=0)
    rank_in_e = jnp.sum(csum * onehot, axis=1) - 1
    counts = csum[-1]
    padded = (counts + MOE_BLOCK - 1) // MOE_BLOCK * MOE_BLOCK
    pad_end = jnp.cumsum(padded)
    pad_start = pad_end - padded
    dest = (jnp.sum(onehot * pad_start[None, :], axis=1) + rank_in_e).astype(I32)
    row_tok = jnp.zeros((n_blocks * MOE_BLOCK,), I32).at[dest].set(jnp.arange(n_assign, dtype=I32) // MOE_TOPK)
    blk = jnp.arange(n_blocks, dtype=I32)
    blk_e = jnp.minimum(jnp.searchsorted(pad_end, blk * MOE_BLOCK, side="right"), n_experts - 1).astype(I32)
    n_used = (pad_end[-1] // MOE_BLOCK).astype(I32)
    prev_e = jnp.concatenate([jnp.full((1,), -1, I32), blk_e[:-1]])
    first = ((blk < n_used) & (blk_e != prev_e)).astype(I32)
    rank = jnp.maximum(jnp.cumsum(first) - 1, 0).astype(I32)
    dist_e = jnp.argsort(counts == 0, stable=True).astype(I32)
    meta = jnp.stack([n_used, jnp.sum(first).astype(I32)])
    return dest, row_tok, (first, rank, dist_e, meta)


def _pool_body(u_ref, st_ref, alias_ref, o_ref, ext, *, q, nb, start_pos, gd):
    del alias_ref
    c = pl.program_id(1)
    head = POOL_BUF + 1

    @pl.when(c == 0)
    def _():
        for i in range(nb):
            ext[i, 1:head] = st_ref[i]

    pos = (start_pos + c * q + lax.broadcasted_iota(I32, (q, 1), 0) + 1).astype(F32)
    for i in range(nb):
        ext[i, head:head + q] = u_ref[i * q:(i + 1) * q]
    for i in range(nb):
        for gi, w in enumerate(POOL_WINDOWS):
            sl = slice(gi * gd, (gi + 1) * gd)
            acc = ext[i, head:head + q, sl]
            cur = acc
            for k in range(1, w):
                acc = acc + ext[i, head - k:head - k + q, sl]
            pooled = acc / jnp.minimum(pos, float(w))
            o_ref[i * q:(i + 1) * q, sl] = (pooled - cur).astype(o_ref.dtype)
    for i in range(nb):
        ext[i, 0:head] = ext[i, q:q + head]


def _pool_mix(u, state, lead, out_buf, row0, n_tiles_outer, nc, q, nb, start_pos):
    t, cdim = u.shape
    gd = cdim // len(POOL_WINDOWS)
    rb0 = row0 // TILE
    nl = len(lead)
    rows = lambda b, c: (rb0 + b * nc + c, 0)
    return pl.pallas_call(
        functools.partial(_pool_body, q=q, nb=nb, start_pos=start_pos, gd=gd),
        out_shape=jax.ShapeDtypeStruct(out_buf.shape, out_buf.dtype),
        grid=(n_tiles_outer, nc),
        in_specs=[pl.BlockSpec((TILE, cdim), rows),
                  pl.BlockSpec((None,) * nl + (nb, POOL_BUF, cdim), lambda b, c: tuple(lead) + (b, 0, 0)),
                  pl.BlockSpec(memory_space=pl.ANY)],
        out_specs=pl.BlockSpec((TILE, cdim), rows),
        scratch_shapes=[pltpu.VMEM((nb, POOL_BUF + 1 + q, cdim), F32)],
        input_output_aliases={2: 0},
        compiler_params=_cparams(("arbitrary", "arbitrary")),
        name="pool_mix",
    )(u, state, out_buf)


def _ssd_body(xs_ref, bm_ref, cm_ref, z_ref, dt_ref, sx_ref, sb_ref, sc_ref, h0_ref,
              wx_ref, wb_ref, wc_ref, bx_ref, bb_ref, bc_ref, dtb_ref, alog_ref, dexp_ref, nw_ref, alias_ref,
              halias_ref, yb_ref, hout_ref, hs, ex, eb, ec, *, q, nb, hpg, kconv):
    del alias_ref, halias_ref
    c = pl.program_id(2)
    p_dim = SSM_HEAD_DIM
    off = SUBLANE - (kconv - 1)

    @pl.when(c == 0)
    def _():
        hs[...] = h0_ref[...].reshape(hs.shape)
        for i in range(nb):
            ex[i, off:SUBLANE] = sx_ref[i]
            eb[i, off:SUBLANE] = sb_ref[i]
            ec[i, off:SUBLANE] = sc_ref[i]

    def conv(raw_ref, e_ref, w_ref, bias_ref):
        outs = []
        for i in range(nb):
            e_ref[i, SUBLANE:SUBLANE + q] = raw_ref[i * q:(i + 1) * q]
            acc = bias_ref[...] + e_ref[i, off:off + q] * w_ref[0:1, :]
            for k in range(1, kconv):
                acc = acc + e_ref[i, off + k:off + k + q] * w_ref[k:k + 1, :]
            outs.append(acc)
            e_ref[i, 0:SUBLANE] = e_ref[i, q:q + SUBLANE]
        o = outs[0] if nb == 1 else jnp.concatenate(outs, axis=0)
        return _silu(o)

    xs = conv(xs_ref, ex, wx_ref, bx_ref)
    bm = conv(bm_ref, eb, wb_ref, bb_ref)
    cm = conv(cm_ref, ec, wc_ref, bc_ref)
    dt = _softplus(dt_ref[...] + dtb_ref[...])
    dta = dt * (-jnp.exp(alog_ref[...]))
    row = lax.broadcasted_iota(I32, (TILE, TILE), 0)
    col = lax.broadcasted_iota(I32, (TILE, TILE), 1)
    same = (row // q) == (col // q)
    causal = same & (col <= row)
    hp = lax.Precision.HIGHEST
    acs = jnp.dot(causal.astype(F32), dta, preferred_element_type=F32, precision=hp)
    tot = jnp.dot(same.astype(F32), dta, preferred_element_type=F32, precision=hp)
    acs_t = acs.T
    tot_t = tot.T
    e_acs = jnp.exp(acs)
    to_end = jnp.exp(tot - acs)
    bm_b = bm.astype(BF16)
    cm_b = cm.astype(BF16)
    scores = lax.dot_general(cm_b, bm_b, _NT, preferred_element_type=F32)
    lo_half = lax.broadcasted_iota(I32, (TILE, LANE), 1) < p_dim

    y_parts, xw_t_parts, ea_parts = [], [], []
    for p in range(hpg // 2):
        j0, j1 = 2 * p, 2 * p + 1

        def expand(v):
            return jnp.where(lo_half, v[:, j0:j0 + 1], v[:, j1:j1 + 1])

        xs_p = xs[:, p * LANE:(p + 1) * LANE]
        xdt = xs_p * expand(dt)
        xdt_b = xdt.astype(BF16)
        ys = []
        for j in (j0, j1):
            decay = jnp.exp(jnp.where(causal, acs[:, j:j + 1] - acs_t[j:j + 1, :], -jnp.inf))
            ys.append(jnp.dot((scores * decay).astype(BF16), xdt_b, preferred_element_type=F32))
        y_parts.append(jnp.where(lo_half, ys[0], ys[1]))
        xw_t_parts.append((xdt * expand(to_end)).T.astype(BF16))
        ea_parts.append(expand(e_acs))
    y_diag = jnp.concatenate(y_parts, axis=1)
    xw_t = jnp.concatenate(xw_t_parts, axis=0)
    ea = jnp.concatenate(ea_parts, axis=1)

    sub = lax.broadcasted_iota(I32, (TILE, 1), 0) // q
    y_off = None
    for i in range(nb):
        h_i = hs[i]
        full = lax.dot_general(cm_b, h_i.astype(BF16), _NT, preferred_element_type=F32)
        y_off = full if nb == 1 else (jnp.where(sub == i, full, 0.0) + (0.0 if y_off is None else y_off))
        bm_i = bm_b if nb == 1 else jnp.where(sub == i, bm, 0.0).astype(BF16)
        st = jnp.dot(xw_t, bm_i, preferred_element_type=F32)
        for j in range(hpg):
            rs = slice(j * p_dim, (j + 1) * p_dim)
            cd = jnp.exp(tot_t[j:j + 1, i * q:i * q + 1])
            hs[i, rs, :] = h_i[rs, :] * cd + st[rs, :]

    y = y_diag + y_off * ea + xs * dexp_ref[...]
    y = y * _silu(z_ref[...])
    ms = jnp.mean(y * y, axis=1, keepdims=True)
    yb_ref[...] = (y * lax.rsqrt(ms + RMS_EPS) * nw_ref[...]).astype(yb_ref.dtype)

    @pl.when(c == pl.num_programs(2) - 1)
    def _():
        hout_ref[...] = hs[...].reshape(hout_ref.shape)


def _ssd_mix(xbc, z, dtr, conv_state, conv_lead, h0, h0_lead, conv_w, conv_b, dtb, alog, dexp, nw, layer,
             out_buf, h_buf, row0, n_outer, nc, q, nb, inner):
    n_state = LANE
    p_dim = SSM_HEAD_DIM
    gw = inner // SSM_GROUPS
    hpg = gw // SSM_HEAD_DIM
    assert hpg % 2 == 0 and gw % LANE == 0
    kconv = conv_w.shape[1]
    rb0 = row0 // TILE
    b_blk = inner // n_state
    c_blk = (inner + SSM_GROUPS * n_state) // n_state
    ncl, nhl = len(conv_lead), len(h0_lead)

    def rows(colf):
        return lambda b, g, c: (rb0 + b * nc + c, colf(g))

    def cstate(width, colf):
        return pl.BlockSpec((None,) * ncl + (nb, kconv - 1, width), lambda b, g, c: tuple(conv_lead) + (b, 0, colf(g)))

    def wspec(nrows, width, colf):
        return pl.BlockSpec((None, nrows, width), lambda b, g, c: (layer, 0, colf(g)))

    xcol = lambda g: g
    bcol = lambda g: b_blk + g
    ccol = lambda g: c_blk + g
    in_specs = [
        pl.BlockSpec((TILE, gw), rows(xcol)), pl.BlockSpec((TILE, n_state), rows(bcol)),
        pl.BlockSpec((TILE, n_state), rows(ccol)), pl.BlockSpec((TILE, gw), rows(xcol)),
        pl.BlockSpec((TILE, LANE), rows(xcol)),
        cstate(gw, xcol), cstate(n_state, bcol), cstate(n_state, ccol),
        pl.BlockSpec((None,) * nhl + (nb, hpg, p_dim, n_state), lambda b, g, c: tuple(h0_lead) + (b, g, 0, 0)),
        wspec(kconv, gw, xcol), wspec(kconv, n_state, bcol), wspec(kconv, n_state, ccol),
        wspec(1, gw, xcol), wspec(1, n_state, bcol), wspec(1, n_state, ccol),
        wspec(1, LANE, xcol), wspec(1, LANE, xcol), wspec(1, gw, xcol), wspec(1, gw, xcol),
        pl.BlockSpec(memory_space=pl.ANY), pl.BlockSpec(memory_space=pl.ANY),
    ]
    return pl.pallas_call(
        functools.partial(_ssd_body, q=q, nb=nb, hpg=hpg, kconv=kconv),
        out_shape=(jax.ShapeDtypeStruct(out_buf.shape, out_buf.dtype),
                   jax.ShapeDtypeStruct(h_buf.shape, h_buf.dtype)),
        grid=(n_outer, SSM_GROUPS, nc),
        in_specs=in_specs,
        out_specs=(pl.BlockSpec((TILE, gw), rows(xcol)),
                   pl.BlockSpec((None, nb, hpg, p_dim, n_state), lambda b, g, c: (layer, b, g, 0, 0))),
        scratch_shapes=[pltpu.VMEM((nb, gw, n_state), F32), pltpu.VMEM((nb, SUBLANE + q, gw), F32),
                        pltpu.VMEM((nb, SUBLANE + q, n_state), F32), pltpu.VMEM((nb, SUBLANE + q, n_state), F32)],
        input_output_aliases={19: 0, 20: 1},
        compiler_params=_cparams(("arbitrary", "arbitrary", "arbitrary")),
        name="ssd_mix",
    )(xbc, xbc, xbc, z, dtr, conv_state, conv_state, conv_state, h0, conv_w, conv_w, conv_w, conv_b, conv_b, conv_b,
      dtb, alog, dexp, nw, out_buf, h_buf)


def _cmlp_body(u_ref, v_ref, ws_ref, bs_ref, g_ref, b_ref, alias_ref, *out_refs, q, n_heads, emit_v):
    del alias_ref
    o_ref = out_refs[0]
    vn = _layer_norm(v_ref[...], g_ref[...], b_ref[...])
    if emit_v:
        out_refs[1][...] = vn
    vb = vn.astype(BF16)
    row = lax.broadcasted_iota(I32, (TILE, TILE), 0)
    col = lax.broadcasted_iota(I32, (TILE, TILE), 1)
    causal = ((row // q) == (col // q)) & (col <= row)
    hd = vb.shape[1] // n_heads
    bs = bs_ref[...]
    for h in range(n_heads):
        sl = slice(h * hd, (h + 1) * hd)
        wm = jnp.where(causal, ws_ref[h], 0.0).astype(BF16)
        mixed = jnp.dot(wm, vb[:, sl], preferred_element_type=F32) + bs[:, h:h + 1]
        o_ref[:, sl] = (u_ref[:, sl].astype(F32) * mixed).astype(o_ref.dtype)


def _cmlp_mix(u, v, ws, ws_lead, bs_t, vg, vb, out_buf, row0, n_tiles, q, emit_v):
    t, cdim = u.shape
    n_heads = ws.shape[-3]
    rb0 = row0 // TILE
    nl = len(ws_lead)
    rows = lambda i: (rb0 + i, 0)
    fixed = lambda i: (0, 0)
    out_shape = [jax.ShapeDtypeStruct(out_buf.shape, out_buf.dtype)]
    out_specs = [pl.BlockSpec((TILE, cdim), rows)]
    if emit_v:
        out_shape.append(jax.ShapeDtypeStruct((n_tiles * TILE, cdim), F32))
        out_specs.append(pl.BlockSpec((TILE, cdim), lambda i: (i, 0)))
    return pl.pallas_call(
        functools.partial(_cmlp_body, q=q, n_heads=n_heads, emit_v=emit_v),
        out_shape=tuple(out_shape),
        grid=(n_tiles,),
        in_specs=[pl.BlockSpec((TILE, cdim), rows), pl.BlockSpec((TILE, cdim), rows),
                  pl.BlockSpec((None,) * nl + (n_heads, TILE, TILE), lambda i: tuple(ws_lead) + (0, 0, 0)),
                  pl.BlockSpec((TILE, LANE), fixed), pl.BlockSpec((1, cdim), fixed), pl.BlockSpec((1, cdim), fixed),
                  pl.BlockSpec(memory_space=pl.ANY)],
        out_specs=tuple(out_specs),
        input_output_aliases={6: 0},
        compiler_params=_cparams(("arbitrary",)),
        name="cmlp_mix",
    )(u, v, ws, bs_t, vg, vb, out_buf)


def _pick(n, target):
    t = min(n, target)
    while n % t:
        t -= SUBLANE
    return t


def kernel(x_prompt, x_sample, state_pool, state_conv, state_ssm, w_in_even, pool_w, pool_scale, conv_w, conv_b, dt_bias, a_log, d_skip, ssm_norm_w, w_out_even, cmlp_w_in, cmlp_b_in, cmlp_v_g, cmlp_v_b, cmlp_w_s, cmlp_b_s, cmlp_w_out, ln_mix_g, ln_mix_b, ln_ffn_g, ln_ffn_b, moe_w_group, moe_w_expert, moe_w_gate, moe_w_up, moe_w_down):
    bp, lp, d = x_prompt.shape
    bs, ls, _ = x_sample.shape
    depth = ln_mix_g.shape[0]
    alpha = float((2 * depth) ** 0.25)
    tp, ts = bp * lp, bs * ls
    t = tp + ts
    pool_dim = state_pool.shape[-1]
    conv_dim = state_conv.shape[-1]
    n_heads_ssm = state_ssm.shape[2]
    inner = n_heads_ssm * SSM_HEAD_DIM
    n_state = state_ssm.shape[-1]
    hpg = n_heads_ssm // SSM_GROUPS
    n_groups = moe_w_group.shape[-1]
    n_experts = moe_w_expert.shape[-1]
    per_group = n_experts // n_groups
    cdim = cmlp_w_out.shape[1]
    c_heads = cmlp_w_s.shape[1]
    assert n_state == LANE and SEQ_CHUNK == TILE and lp % TILE == 0 and TILE % ls == 0 and ts % TILE == 0
    assert n_groups + n_experts <= LANE and hpg <= LANE
    q_s = math.gcd(ls, SEQ_CHUNK)
    assert q_s == ls
    nb_s = TILE // q_s
    n_blocks = -(-(t * MOE_TOPK) // MOE_BLOCK) + n_experts
    n_rows = n_blocks * MOE_BLOCK

    tm = _pick(t, 1024)
    tm_ln = _pick(t, 256)
    tn = 512

    x = jnp.concatenate([x_prompt.reshape(tp, d), x_sample.reshape(ts, d)], axis=0)
    xb = x.astype(BF16)

    dt_col = pool_dim + inner + conv_dim

    def group_pad(v):
        lead = v.shape[:-1]
        v = v.reshape(lead + (SSM_GROUPS, hpg))
        v = jnp.pad(v, [(0, 0)] * len(lead) + [(0, 0), (0, LANE - hpg)])
        return v.reshape(lead + (SSM_GROUPS * LANE,))

    w_dt = group_pad(w_in_even[:, :, dt_col:])
    dtb_p = group_pad(dt_bias)[:, None, :]
    alog_p = group_pad(a_log)[:, None, :]
    dexp = jnp.repeat(d_skip, SSM_HEAD_DIM, axis=-1)[:, None, :]
    nw3 = ssm_norm_w[:, None, :]
    conv_b3 = conv_b[:, None, :]
    w_router = jnp.concatenate(
        [moe_w_group, moe_w_expert, jnp.zeros((depth, d, LANE - n_groups - n_experts), F32)], axis=-1).astype(BF16)
    zero_pool = jnp.zeros((bp, POOL_BUF, pool_dim), F32)
    zero_conv = jnp.zeros((bp, state_conv.shape[2], conv_dim), F32)
    n_even = state_ssm.shape[0]
    zero_ssm = jnp.zeros((bp,) + state_ssm.shape[2:], F32)
    ssm_p = jnp.zeros((n_even, bp) + state_ssm.shape[2:], F32)
    ssm_s = jnp.zeros(state_ssm.shape, F32)
    eye = jnp.eye(nb_s, dtype=F32)
    ws_s = jnp.einsum("ab,lhts->lhatbs", eye, cmlp_w_s[:, :, :q_s, :q_s]).reshape(-1, c_heads, TILE, TILE)
    bs_p = jnp.pad(jnp.swapaxes(cmlp_b_s, 1, 2), ((0, 0), (0, 0), (0, LANE - c_heads)))
    bs_s = jnp.tile(bs_p[:, :q_s], (1, nb_s, 1))

    def last_rows(a, row0, n_seq, seq_len, n):
        return jnp.stack([a[row0 + (s + 1) * seq_len - n:row0 + (s + 1) * seq_len] for s in range(n_seq)])

    def new_state(old, a, n):
        cur = a[tp:].reshape(bs, ls, a.shape[1])
        ext = cur if ls >= n else jnp.concatenate([old[:, ls:], cur], axis=1)
        return last_rows(a, 0, bp, lp, n), ext[:, ext.shape[1] - n:]

    pool_out, conv_out, v_out = [], [], []
    for layer in range(depth):
        i = layer // 2
        if layer % 2 == 0:
            lhs = [(xb, 0)]
            u_a = _matmul(lhs, w_in_even, (i,), (0,), d, 0, pool_dim, tn, tm, F32, name="in_pool")
            z = _matmul(lhs, w_in_even, (i,), (0,), d, pool_dim, inner, tn, tm, F32, name="in_gate")
            xbc = _matmul(lhs, w_in_even, (i,), (0,), d, pool_dim + inner, conv_dim, tn, tm, F32, name="in_conv")
            dtr = _matmul(lhs, w_dt, (i,), (0,), d, 0, SSM_GROUPS * LANE, tn, tm, F32, name="in_dt")

            mixed = jnp.zeros((t, pool_dim), BF16)
            mixed = _pool_mix(u_a, zero_pool, (), mixed, 0, bp, lp // TILE, TILE, 1, 0)
            mixed = _pool_mix(u_a, state_pool, (i,), mixed, tp, ts // TILE, 1, q_s, nb_s, PAST_LEN)
            y_a = _group_matmul(mixed, pool_w, i, pool_scale[i][None, :], tm)

            y_b = jnp.zeros((t, inner), BF16)
            y_b, ssm_p = _ssd_mix(xbc, z, dtr, zero_conv, (), zero_ssm, (), conv_w, conv_b3, dtb_p, alog_p, dexp, nw3,
                                  i, y_b, ssm_p, 0, bp, lp // TILE, TILE, 1, inner)
            y_b, ssm_s = _ssd_mix(xbc, z, dtr, state_conv, (i,), state_ssm, (i,), conv_w, conv_b3, dtb_p, alog_p,
                                  dexp, nw3, i, y_b, ssm_s, tp, ts // TILE, 1, q_s, nb_s, inner)

            kc = pool_dim
            assert inner % kc == 0
            lhs_out = [(y_a, 0)] + [(y_b, k) for k in range(inner // kc)]
            mix = _matmul(lhs_out, w_out_even, (i,), tuple(range(1 + inner // kc)), kc, 0, d, tn, _pick(t, 256), F32,
                          name="out_even")

            pool_out.append(new_state(state_pool[i], u_a, POOL_BUF))
            conv_out.append(new_state(state_conv[i], xbc, state_conv.shape[2]))
        else:
            lhs = [(xb, 0)]
            bias = cmlp_b_in[i][None, :]
            u = _matmul(lhs, cmlp_w_in, (i,), (0,), d, 0, cdim, tn, tm, F32, bias=bias[:, :cdim], act="gelu",
                        name="cmlp_in_u")
            v = _matmul(lhs, cmlp_w_in, (i,), (0,), d, cdim, cdim, tn, tm, F32, bias=bias[:, cdim:], act="gelu",
                        name="cmlp_in_v")
            vg, vb_ = cmlp_v_g[i][None, :], cmlp_v_b[i][None, :]
            gated = jnp.zeros((t, cdim), BF16)
            (gated,) = _cmlp_mix(u, v, cmlp_w_s, (i,), bs_p[i], vg, vb_, gated, 0, tp // TILE, SEQ_CHUNK, False)
            gated, v_n = _cmlp_mix(u, v, ws_s, (i,), bs_s[i], vg, vb_, gated, tp, ts // TILE, q_s, True)
            v_out.append(v_n.reshape(bs, ls, cdim))
            mix = _matmul([(gated, 0)], cmlp_w_out, (i,), (0,), cdim, 0, d, tn, tm, F32, name="cmlp_out")

        x, xp, rt = _ln_router(x, mix, ln_mix_g[layer][None, :], ln_mix_b[layer][None, :], w_router[layer], alpha,
                               n_groups, per_group, tm_ln)
        dest, row_tok, plan = _moe_plan(rt, n_experts, n_blocks)
        xs = _dispatch(row_tok, plan[3], xp)
        hdim = moe_w_gate.shape[-1]
        h = _expert_up(plan, xs, moe_w_gate, moe_w_up, layer, min(hdim, 512))
        y_rows = _expert_down(plan, h, moe_w_down, layer, min(d, 2048))
        x, xb = _combine_ln(dest, x, rt, ln_ffn_g[layer][None, :], ln_ffn_b[layer][None, :], y_rows, alpha, tm_ln)

    y_prompt = x[:tp].reshape(bp, lp, d)
    y_sample = x[tp:].reshape(bs, ls, d)
    pool_p = jnp.stack([a for a, _ in pool_out])
    pool_s = jnp.stack([b for _, b in pool_out])
    conv_p = jnp.stack([a for a, _ in conv_out])
    conv_s = jnp.stack([b for _, b in conv_out])
    v_s = jnp.stack(v_out)
    return (y_prompt, y_sample, pool_p, pool_s, conv_p, conv_s, ssm_p, ssm_s, v_s)
```

```python
import functools
import math

import jax
import jax.numpy as jnp
import numpy as np
from jax import lax
from jax.experimental import pallas as pl
from jax.experimental.pallas import tpu as pltpu

F32 = jnp.float32
BF16 = jnp.bfloat16
U32 = jnp.uint32
I32 = jnp.int32

POOL_WINDOWS = (2, 4, 8, 16)
POOL_BUF = max(POOL_WINDOWS) - 1
SSM_HEAD_DIM = 64
SSM_GROUPS = 8
SEQ_CHUNK = 128
MOE_TOPK = 2
PAST_LEN = 16384
LN_EPS = 1e-5
RMS_EPS = 1e-5

LANE = 128
SUBLANE = 8
V7X_VMEM_BYTES = 64 * 1024 * 1024
VMEM_LIMIT = V7X_VMEM_BYTES - 8 * 1024 * 1024
TILE = 128
MOE_BLOCK = 256

_NT = (((1,), (1,)), ((), ()))


def _cparams(sem):
    return pltpu.CompilerParams(dimension_semantics=sem, vmem_limit_bytes=VMEM_LIMIT)


def _sigmoid(x):
    return 0.5 * (jnp.tanh(0.5 * x) + 1.0)


def _silu(x):
    return x * _sigmoid(x)


def _gelu_tanh(x):
    return 0.5 * x * (1.0 + jnp.tanh(math.sqrt(2.0 / math.pi) * (x + 0.044715 * (x * x * x))))


def _softplus(x):
    return jnp.maximum(x, 0.0) + jnp.log1p(jnp.exp(-jnp.abs(x)))


def _layer_norm(y, g, b):
    mu = jnp.mean(y, axis=-1, keepdims=True)
    d = y - mu
    var = jnp.mean(d * d, axis=-1, keepdims=True)
    return d * lax.rsqrt(var + LN_EPS) * g + b


def _mm_body(*refs, n_chunks, has_bias, has_scale, act):
    xs = refs[:n_chunks]
    ws = refs[n_chunks:2 * n_chunks]
    pos = 2 * n_chunks
    bias_ref = scale_ref = None
    if has_bias:
        bias_ref = refs[pos]
        pos += 1
    if has_scale:
        scale_ref = refs[pos]
        pos += 1
    o_ref, wb = refs[pos], refs[pos + 1]

    @pl.when(pl.program_id(1) == 0)
    def _():
        for c in range(n_chunks):
            wb[c] = ws[c][...].astype(BF16)

    acc = None
    for c in range(n_chunks):
        d = jnp.dot(xs[c][...], wb[c], preferred_element_type=F32)
        acc = d if acc is None else acc + d
    if has_scale:
        acc = acc * scale_ref[...]
    if has_bias:
        acc = acc + bias_ref[...]
    if act == "gelu":
        acc = _gelu_tanh(acc)
    o_ref[...] = acc.astype(o_ref.dtype)


def _matmul(lhs, w, w_lead, w_row_blocks, kc, col0, n_cols, tn, tm, out_dtype, bias=None, act=None, name="mm"):
    n_chunks = len(lhs)
    m = lhs[0][0].shape[0]
    assert m % tm == 0 and n_cols % tn == 0 and col0 % tn == 0
    cb0 = col0 // tn
    nl = len(w_lead)
    in_specs = []
    for _, cb in lhs:
        in_specs.append(pl.BlockSpec((tm, kc), functools.partial(lambda j, i, cb: (i, cb), cb=cb)))
    for rb in w_row_blocks:
        in_specs.append(pl.BlockSpec((None,) * nl + (kc, tn),
                                     functools.partial(lambda j, i, rb: tuple(w_lead) + (rb, cb0 + j), rb=rb)))
    args = [a for a, _ in lhs] + [w] * n_chunks
    if bias is not None:
        in_specs.append(pl.BlockSpec((1, tn), lambda j, i: (0, j)))
        args.append(bias)
    return pl.pallas_call(
        functools.partial(_mm_body, n_chunks=n_chunks, has_bias=bias is not None, has_scale=False, act=act),
        out_shape=jax.ShapeDtypeStruct((m, n_cols), out_dtype),
        grid=(n_cols // tn, m // tm),
        in_specs=in_specs,
        out_specs=pl.BlockSpec((tm, tn), lambda j, i: (i, j)),
        scratch_shapes=[pltpu.VMEM((n_chunks, kc, tn), BF16)],
        compiler_params=_cparams(("arbitrary", "arbitrary")),
        name=name,
    )(*args)


def _group_matmul(x, w, layer, scale, tm):
    m = x.shape[0]
    _, n_g, gd, _ = w.shape
    return pl.pallas_call(
        functools.partial(_mm_body, n_chunks=1, has_bias=False, has_scale=True, act=None),
        out_shape=jax.ShapeDtypeStruct((m, n_g * gd), BF16),
        grid=(n_g, m // tm),
        in_specs=[pl.BlockSpec((tm, gd), lambda g, i: (i, g)),
                  pl.BlockSpec((None, None, gd, gd), lambda g, i: (layer, g, 0, 0)),
                  pl.BlockSpec((1, gd), lambda g, i: (0, g))],
        out_specs=pl.BlockSpec((tm, gd), lambda g, i: (i, g)),
        scratch_shapes=[pltpu.VMEM((1, gd, gd), BF16)],
        compiler_params=_cparams(("arbitrary", "arbitrary")),
        name="pool_proj",
    )(x, w, scale)


def _pack_bf16_pair(y):
    half = y.shape[1] // 2
    hi = lax.bitcast_convert_type(y[:, :half].astype(BF16).astype(F32), U32)
    lo = lax.bitcast_convert_type(y[:, half:].astype(BF16).astype(F32), U32)
    return (hi & jnp.uint32(0xFFFF0000)) | (lo >> 16)


def _unpack_bf16_pair(p):
    hi = lax.bitcast_convert_type(p & jnp.uint32(0xFFFF0000), F32).astype(BF16)
    lo = lax.bitcast_convert_type(p << 16, F32).astype(BF16)
    return hi, lo


def _ln_router_body(x_ref, mix_ref, g_ref, b_ref, wr_ref, xo_ref, xp_ref, rt_ref, *, alpha, n_groups, per_group):
    y = _layer_norm(alpha * x_ref[...] + mix_ref[...], g_ref[...], b_ref[...])
    xo_ref[...] = y
    xp_ref[...] = _pack_bf16_pair(y)
    logits = jnp.dot(y.astype(BF16), wr_ref[...], preferred_element_type=F32)
    rows = logits.shape[0]
    lane = lax.broadcasted_iota(I32, (rows, LANE), 1).astype(F32)
    neg = -jnp.inf
    far = float(LANE)
    is_g = lane < n_groups
    gl = jnp.where(is_g, logits, neg)
    gmax = jnp.max(gl, axis=1, keepdims=True)
    gsel = jnp.min(jnp.where(gl == gmax, lane, far), axis=1, keepdims=True)
    gden = jnp.sum(jnp.where(is_g, jnp.exp(jnp.where(is_g, logits, gmax) - gmax), 0.0), axis=1, keepdims=True)
    gw = 1.0 / gden
    lo = n_groups + gsel * per_group
    el = jnp.where((lane >= lo) & (lane < lo + per_group), logits, neg)
    e1 = jnp.max(el, axis=1, keepdims=True)
    i1 = jnp.min(jnp.where(el == e1, lane, far), axis=1, keepdims=True)
    el2 = jnp.where(lane == i1, neg, el)
    e2 = jnp.max(el2, axis=1, keepdims=True)
    i2 = jnp.min(jnp.where(el2 == e2, lane, far), axis=1, keepdims=True)
    t = jnp.exp(e2 - e1)
    w1 = gw / (1.0 + t)
    w2 = gw * t / (1.0 + t)
    out = jnp.where(lane == 0, i1 - n_groups,
                    jnp.where(lane == 1, i2 - n_groups,
                              jnp.where(lane == 2, w1, jnp.where(lane == 3, w2, 0.0))))
    rt_ref[...] = out


def _ln_router(x, mix, g, b, w_router, alpha, n_groups, per_group, tm):
    t, d = x.shape
    row = lambda i: (i, 0)
    fixed = lambda i: (0, 0)
    return pl.pallas_call(
        functools.partial(_ln_router_body, alpha=alpha, n_groups=n_groups, per_group=per_group),
        out_shape=(jax.ShapeDtypeStruct((t, d), F32), jax.ShapeDtypeStruct((t, d // 2), U32),
                   jax.ShapeDtypeStruct((t, LANE), F32)),
        grid=(t // tm,),
        in_specs=[pl.BlockSpec((tm, d), row), pl.BlockSpec((tm, d), row), pl.BlockSpec((1, d), fixed),
                  pl.BlockSpec((1, d), fixed), pl.BlockSpec((d, LANE), fixed)],
        out_specs=(pl.BlockSpec((tm, d), row), pl.BlockSpec((tm, d // 2), row), pl.BlockSpec((tm, LANE), row)),
        compiler_params=_cparams(("arbitrary",)),
        name="ln_router",
    )(x, mix, g, b, w_router)


def _row_gather(n, src_hbm, src_row, dst, sem, wait):
    def body(g, carry):
        for u in range(SUBLANE):
            cp = pltpu.make_async_copy(src_hbm.at[pl.ds(src_row(g * SUBLANE + u), 1)], dst.at[g, pl.ds(u, 1)], sem)
            if wait:
                cp.wait()
            else:
                cp.start()
        return carry

    lax.fori_loop(0, n // SUBLANE, body, 0)


def _dispatch_body(tok_ref, meta_ref, xp_hbm, o_ref, buf, sems):
    b = pl.program_id(0)
    n_used = meta_ref[0]

    def rows(blk, wait):
        slot = blk % 2
        _row_gather(MOE_BLOCK, xp_hbm, lambda r: tok_ref[blk * MOE_BLOCK + r], buf.at[slot], sems.at[slot], wait)

    @pl.when((b == 0) & (n_used > 0))
    def _():
        rows(b, False)

    @pl.when(b + 1 < n_used)
    def _():
        rows(b + 1, False)

    @pl.when(b < n_used)
    def _():
        rows(b, True)
        o_ref[...] = buf[b % 2].reshape(o_ref.shape)

    @pl.when(b >= n_used)
    def _():
        o_ref[...] = jnp.zeros_like(o_ref)


def _dispatch(row_tok, meta, xp):
    dh = xp.shape[1]
    n_rows = row_tok.shape[0]
    return pl.pallas_call(
        _dispatch_body,
        out_shape=jax.ShapeDtypeStruct((n_rows, dh), U32),
        grid_spec=pltpu.PrefetchScalarGridSpec(
            num_scalar_prefetch=2,
            grid=(n_rows // MOE_BLOCK,),
            in_specs=[pl.BlockSpec(memory_space=pl.ANY)],
            out_specs=pl.BlockSpec((MOE_BLOCK, dh), lambda b, tok, meta: (b, 0)),
            scratch_shapes=[pltpu.VMEM((2, MOE_BLOCK // SUBLANE, SUBLANE, dh), U32), pltpu.SemaphoreType.DMA((2,))],
        ),
        compiler_params=_cparams(("arbitrary",)),
        name="moe_dispatch",
    )(row_tok, meta, xp)


def _weight_jobs(first_ref, rank_ref, dist_ref, meta_ref, w_hbms, layer, chunk, wbuf, sems):
    p, b = pl.program_id(0), pl.program_id(1)
    n_phase = pl.num_programs(0)
    nd = meta_ref[1]
    r = rank_ref[b]
    is_first = first_ref[b] == 1
    slot = (p * nd + r) & 1

    def copies(phase, rnk, slot_):
        e = dist_ref[rnk]
        col = pl.multiple_of(phase * chunk, chunk)
        return [pltpu.make_async_copy(w.at[layer, e, :, pl.ds(col, chunk)], wbuf.at[slot_, i], sems.at[slot_])
                for i, w in enumerate(w_hbms)]

    @pl.when(is_first & (p == 0) & (r == 0))
    def _():
        for cp in copies(p, r, slot):
            cp.start()

    wrap = r + 1 >= nd
    nxt_p = jnp.where(wrap, p + 1, p)
    nxt_r = jnp.where(wrap, 0, r + 1)

    @pl.when(is_first & (nxt_p < n_phase))
    def _():
        for cp in copies(nxt_p, nxt_r, 1 - slot):
            cp.start()

    @pl.when(is_first)
    def _():
        for cp in copies(p, r, slot):
            cp.wait()

    return is_first, slot


def _expert_up_body(first_ref, rank_ref, dist_ref, meta_ref, xs_ref, wg_hbm, wu_hbm, h_ref, wbuf, wgb, wub, sems,
                    *, layer, hc):
    is_first, slot = _weight_jobs(first_ref, rank_ref, dist_ref, meta_ref, (wg_hbm, wu_hbm), layer, hc, wbuf, sems)
    blk = pl.program_id(1)
    n_used = meta_ref[0]

    @pl.when(is_first)
    def _():
        wgb[...] = wbuf[slot, 0].astype(BF16)
        wub[...] = wbuf[slot, 1].astype(BF16)

    @pl.when(blk < n_used)
    def _():
        hi, lo = _unpack_bf16_pair(xs_ref[...])
        half = hi.shape[1]
        gate = (jnp.dot(hi, wgb[:half], preferred_element_type=F32)
                + jnp.dot(lo, wgb[half:], preferred_element_type=F32))
        up = (jnp.dot(hi, wub[:half], preferred_element_type=F32)
              + jnp.dot(lo, wub[half:], preferred_element_type=F32))
        h_ref[...] = (_silu(gate) * up).astype(BF16)

    @pl.when(blk >= n_used)
    def _():
        h_ref[...] = jnp.zeros_like(h_ref)


def _expert_up(plan, xs, w_gate, w_up, layer, hc):
    n_rows, dh = xs.shape
    d, hdim = w_gate.shape[2], w_gate.shape[3]
    n_blk = n_rows // MOE_BLOCK
    return pl.pallas_call(
        functools.partial(_expert_up_body, layer=layer, hc=hc),
        out_shape=jax.ShapeDtypeStruct((n_rows, hdim), BF16),
        grid_spec=pltpu.PrefetchScalarGridSpec(
            num_scalar_prefetch=4,
            grid=(hdim // hc, n_blk),
            in_specs=[pl.BlockSpec((MOE_BLOCK, dh), lambda p, b, *_: (b, 0)),
                      pl.BlockSpec(memory_space=pl.ANY), pl.BlockSpec(memory_space=pl.ANY)],
            out_specs=pl.BlockSpec((MOE_BLOCK, hc), lambda p, b, *_: (b, p)),
            scratch_shapes=[pltpu.VMEM((2, 2, d, hc), F32), pltpu.VMEM((d, hc), BF16), pltpu.VMEM((d, hc), BF16),
                            pltpu.SemaphoreType.DMA((2,))],
        ),
        compiler_params=_cparams(("arbitrary", "arbitrary")),
        name="expert_up",
    )(*plan, xs, w_gate, w_up)


def _expert_down_body(first_ref, rank_ref, dist_ref, meta_ref, h_ref, wd_hbm, y_ref, wbuf, wdb, sems, *, layer, oc):
    is_first, slot = _weight_jobs(first_ref, rank_ref, dist_ref, meta_ref, (wd_hbm,), layer, oc, wbuf, sems)
    blk = pl.program_id(1)
    n_used = meta_ref[0]

    @pl.when(is_first)
    def _():
        wdb[...] = wbuf[slot, 0].astype(BF16)

    @pl.when(blk < n_used)
    def _():
        y_ref[...] = jnp.dot(h_ref[...], wdb[...], preferred_element_type=F32)

    @pl.when(blk >= n_used)
    def _():
        y_ref[...] = jnp.zeros_like(y_ref)


def _expert_down(plan, h, w_down, layer, oc):
    n_rows, hdim = h.shape
    d = w_down.shape[3]
    n_blk = n_rows // MOE_BLOCK
    return pl.pallas_call(
        functools.partial(_expert_down_body, layer=layer, oc=oc),
        out_shape=jax.ShapeDtypeStruct((n_rows, d), F32),
        grid_spec=pltpu.PrefetchScalarGridSpec(
            num_scalar_prefetch=4,
            grid=(d // oc, n_blk),
            in_specs=[pl.BlockSpec((MOE_BLOCK, hdim), lambda q, b, *_: (b, 0)), pl.BlockSpec(memory_space=pl.ANY)],
            out_specs=pl.BlockSpec((MOE_BLOCK, oc), lambda q, b, *_: (b, q)),
            scratch_shapes=[pltpu.VMEM((2, 1, hdim, oc), F32), pltpu.VMEM((hdim, oc), BF16),
                            pltpu.SemaphoreType.DMA((2,))],
        ),
        compiler_params=_cparams(("arbitrary", "arbitrary")),
        name="expert_down",
    )(*plan, h, w_down)


def _combine_ln_body(dest_ref, x_ref, rt_ref, g_ref, b_ref, y_hbm, xo_ref, xb_ref, rows, sems, *, alpha, tm):
    s = pl.program_id(0)

    def gather(step, wait):
        slot = step % 2
        for k in range(MOE_TOPK):
            _row_gather(tm, y_hbm, lambda r: dest_ref[(step * tm + r) * MOE_TOPK + k], rows.at[slot, k],
                        sems.at[slot], wait)

    @pl.when(s == 0)
    def _():
        gather(s, False)

    @pl.when(s + 1 < pl.num_programs(0))
    def _():
        gather(s + 1, False)

    gather(s, True)
    slot = s % 2
    rt = rt_ref[...]
    x = x_ref[...]
    ffn = rt[:, 2:3] * rows[slot, 0].reshape(x.shape) + rt[:, 3:4] * rows[slot, 1].reshape(x.shape)
    y = _layer_norm(alpha * x + ffn, g_ref[...], b_ref[...])
    xo_ref[...] = y
    xb_ref[...] = y.astype(BF16)


def _combine_ln(dest, x, rt, g, b, y_rows, alpha, tm):
    t, d = x.shape
    row = lambda i, de: (i, 0)
    fixed = lambda i, de: (0, 0)
    return pl.pallas_call(
        functools.partial(_combine_ln_body, alpha=alpha, tm=tm),
        out_shape=(jax.ShapeDtypeStruct((t, d), F32), jax.ShapeDtypeStruct((t, d), BF16)),
        grid_spec=pltpu.PrefetchScalarGridSpec(
            num_scalar_prefetch=1,
            grid=(t // tm,),
            in_specs=[pl.BlockSpec((tm, d), row), pl.BlockSpec((tm, LANE), row), pl.BlockSpec((1, d), fixed),
                      pl.BlockSpec((1, d), fixed), pl.BlockSpec(memory_space=pl.ANY)],
            out_specs=(pl.BlockSpec((tm, d), row), pl.BlockSpec((tm, d), row)),
            scratch_shapes=[pltpu.VMEM((2, MOE_TOPK, tm // SUBLANE, SUBLANE, d), F32), pltpu.SemaphoreType.DMA((2,))],
        ),
        compiler_params=_cparams(("arbitrary",)),
        name="moe_combine_ln",
    )(dest, x, rt, g, b, y_rows)


def _moe_plan(rt, n_experts, n_blocks):
    n_assign = rt.shape[0] * MOE_TOPK
    e = rt[:, :MOE_TOPK].astype(I32).reshape(-1)
    onehot = (e[:, None] == jnp.arange(n_experts, dtype=I32)[None, :]).astype(I32)
    csum = jnp.cumsum(onehot, axis=0)
    rank_in_e = jnp.sum(csum * onehot, axis=1) - 1
    counts = csum[-1]
    padded = (counts + MOE_BLOCK - 1) // MOE_BLOCK * MOE_BLOCK
    pad_end = jnp.cumsum(padded)
    pad_start = pad_end - padded
    dest = (jnp.sum(onehot * pad_start[None, :], axis=1) + rank_in_e).astype(I32)
    row_tok = jnp.zeros((n_blocks * MOE_BLOCK,), I32).at[dest].set(jnp.arange(n_assign, dtype=I32) // MOE_TOPK)
    blk = jnp.arange(n_blocks, dtype=I32)
    blk_e = jnp.minimum(jnp.searchsorted(pad_end, blk * MOE_BLOCK, side="right"), n_experts - 1).astype(I32)
    n_used = (pad_end[-1] // MOE_BLOCK).astype(I32)
    prev_e = jnp.concatenate([jnp.full((1,), -1, I32), blk_e[:-1]])
    first = ((blk < n_used) & (blk_e != prev_e)).astype(I32)
    rank = jnp.maximum(jnp.cumsum(first) - 1, 0).astype(I32)
    dist_e = jnp.argsort(counts == 0, stable=True).astype(I32)
    meta = jnp.stack([n_used, jnp.sum(first).astype(I32)])
    return dest, row_tok, (first, rank, dist_e, meta)


def _pool_body(u_ref, st_ref, alias_ref, o_ref, ext, *, q, nb, start_pos, gd):
    del alias_ref
    c = pl.program_id(1)
    head = POOL_BUF + 1

    @pl.when(c == 0)
    def _():
        for i in range(nb):
            ext[i, 1:head] = st_ref[i]

    pos = (start_pos + c * q + lax.broadcasted_iota(I32, (q, 1), 0) + 1).astype(F32)
    for i in range(nb):
        ext[i, head:head + q] = u_ref[i * q:(i + 1) * q]
    for i in range(nb):
        for gi, w in enumerate(POOL_WINDOWS):
            sl = slice(gi * gd, (gi + 1) * gd)
            acc = ext[i, head:head + q, sl]
            cur = acc
            for k in range(1, w):
                acc = acc + ext[i, head - k:head - k + q, sl]
            pooled = acc / jnp.minimum(pos, float(w))
            o_ref[i * q:(i + 1) * q, sl] = (pooled - cur).astype(o_ref.dtype)
    for i in range(nb):
        ext[i, 0:head] = ext[i, q:q + head]


def _pool_mix(u, state, lead, out_buf, row0, n_tiles_outer, nc, q, nb, start_pos):
    t, cdim = u.shape
    gd = cdim // len(POOL_WINDOWS)
    rb0 = row0 // TILE
    nl = len(lead)
    rows = lambda b, c: (rb0 + b * nc + c, 0)
    return pl.pallas_call(
        functools.partial(_pool_body, q=q, nb=nb, start_pos=start_pos, gd=gd),
        out_shape=jax.ShapeDtypeStruct(out_buf.shape, out_buf.dtype),
        grid=(n_tiles_outer, nc),
        in_specs=[pl.BlockSpec((TILE, cdim), rows),
                  pl.BlockSpec((None,) * nl + (nb, POOL_BUF, cdim), lambda b, c: tuple(lead) + (b, 0, 0)),
                  pl.BlockSpec(memory_space=pl.ANY)],
        out_specs=pl.BlockSpec((TILE, cdim), rows),
        scratch_shapes=[pltpu.VMEM((nb, POOL_BUF + 1 + q, cdim), F32)],
        input_output_aliases={2: 0},
        compiler_params=_cparams(("arbitrary", "arbitrary")),
        name="pool_mix",
    )(u, state, out_buf)


def _ssd_body(xs_ref, bm_ref, cm_ref, z_ref, dt_ref, sx_ref, sb_ref, sc_ref, h0_ref,
              wx_ref, wb_ref, wc_ref, bx_ref, bb_ref, bc_ref, dtb_ref, alog_ref, dexp_ref, nw_ref, sel_ref,
              alias_ref, halias_ref, yb_ref, hout_ref, hs, ex, eb, ec, *, q, nb, hpg, kconv):
    del alias_ref, halias_ref
    c = pl.program_id(2)
    p_dim = SSM_HEAD_DIM
    off = SUBLANE - (kconv - 1)

    @pl.when(c == 0)
    def _():
        hs[...] = h0_ref[...].reshape(hs.shape)
        for i in range(nb):
            ex[i, off:SUBLANE] = sx_ref[i]
            eb[i, off:SUBLANE] = sb_ref[i]
            ec[i, off:SUBLANE] = sc_ref[i]

    def conv(raw_ref, e_ref, w_ref, bias_ref):
        outs = []
        for i in range(nb):
            e_ref[i, SUBLANE:SUBLANE + q] = raw_ref[i * q:(i + 1) * q]
            acc = bias_ref[...] + e_ref[i, off:off + q] * w_ref[0:1, :]
            for k in range(1, kconv):
                acc = acc + e_ref[i, off + k:off + k + q] * w_ref[k:k + 1, :]
            outs.append(acc)
            e_ref[i, 0:SUBLANE] = e_ref[i, q:q + SUBLANE]
        o = outs[0] if nb == 1 else jnp.concatenate(outs, axis=0)
        return _silu(o)

    xs = conv(xs_ref, ex, wx_ref, bx_ref)
    bm = conv(bm_ref, eb, wb_ref, bb_ref)
    cm = conv(cm_ref, ec, wc_ref, bc_ref)
    dt_raw = jnp.dot(dt_ref[...], sel_ref[...], preferred_element_type=F32, precision=lax.Precision.HIGHEST)
    dt = _softplus(dt_raw + dtb_ref[...])
    dta = dt * (-jnp.exp(alog_ref[...]))
    row = lax.broadcasted_iota(I32, (TILE, TILE), 0)
    col = lax.broadcasted_iota(I32, (TILE, TILE), 1)
    same = (row // q) == (col // q)
    causal = same & (col <= row)
    hp = lax.Precision.HIGHEST
    acs = jnp.dot(causal.astype(F32), dta, preferred_element_type=F32, precision=hp)
    tot = jnp.dot(same.astype(F32), dta, preferred_element_type=F32, precision=hp)
    acs_t = acs.T
    tot_t = tot.T
    e_acs = jnp.exp(acs)
    to_end = jnp.exp(tot - acs)
    bm_b = bm.astype(BF16)
    cm_b = cm.astype(BF16)
    scores = lax.dot_general(cm_b, bm_b, _NT, preferred_element_type=F32)
    lo_half = lax.broadcasted_iota(I32, (TILE, LANE), 1) < p_dim

    y_parts, xw_t_parts, ea_parts = [], [], []
    for p in range(hpg // 2):
        j0, j1 = 2 * p, 2 * p + 1

        def expand(v):
            return jnp.where(lo_half, v[:, j0:j0 + 1], v[:, j1:j1 + 1])

        xs_p = xs[:, p * LANE:(p + 1) * LANE]
        xdt = xs_p * expand(dt)
        xdt_b = xdt.astype(BF16)
        ys = []
        for j in (j0, j1):
            decay = jnp.exp(jnp.where(causal, acs[:, j:j + 1] - acs_t[j:j + 1, :], -jnp.inf))
            ys.append(jnp.dot((scores * decay).astype(BF16), xdt_b, preferred_element_type=F32))
        y_parts.append(jnp.where(lo_half, ys[0], ys[1]))
        xw_t_parts.append((xdt * expand(to_end)).T.astype(BF16))
        ea_parts.append(expand(e_acs))
    y_diag = jnp.concatenate(y_parts, axis=1)
    xw_t = jnp.concatenate(xw_t_parts, axis=0)
    ea = jnp.concatenate(ea_parts, axis=1)

    sub = lax.broadcasted_iota(I32, (TILE, 1), 0) // q
    y_off = None
    for i in range(nb):
        h_i = hs[i]
        full = lax.dot_general(cm_b, h_i.astype(BF16), _NT, preferred_element_type=F32)
        y_off = full if nb == 1 else (jnp.where(sub == i, full, 0.0) + (0.0 if y_off is None else y_off))
        bm_i = bm_b if nb == 1 else jnp.where(sub == i, bm, 0.0).astype(BF16)
        st = jnp.dot(xw_t, bm_i, preferred_element_type=F32)
        for j in range(hpg):
            rs = slice(j * p_dim, (j + 1) * p_dim)
            cd = jnp.exp(tot_t[j:j + 1, i * q:i * q + 1])
            hs[i, rs, :] = h_i[rs, :] * cd + st[rs, :]

    y = y_diag + y_off * ea + xs * dexp_ref[...]
    y = y * _silu(z_ref[...])
    ms = jnp.mean(y * y, axis=1, keepdims=True)
    yb_ref[...] = (y * lax.rsqrt(ms + RMS_EPS) * nw_ref[...]).astype(yb_ref.dtype)

    @pl.when(c == pl.num_programs(2) - 1)
    def _():
        hout_ref[...] = hs[...].reshape(hout_ref.shape)


def _ssd_mix(xbc, z, dtr, conv_state, conv_lead, h0, h0_lead, conv_w, conv_b, dtb, alog, dexp, nw, layer,
             out_buf, h_buf, row0, n_outer, nc, q, nb, inner):
    n_state = LANE
    p_dim = SSM_HEAD_DIM
    gw = inner // SSM_GROUPS
    hpg = gw // SSM_HEAD_DIM
    assert hpg % 2 == 0 and gw % LANE == 0
    kconv = conv_w.shape[1]
    rb0 = row0 // TILE
    b_blk = inner // n_state
    c_blk = (inner + SSM_GROUPS * n_state) // n_state
    ncl, nhl = len(conv_lead), len(h0_lead)
    sel = np.zeros((SSM_GROUPS, LANE, LANE), np.float32)
    for g in range(SSM_GROUPS):
        sel[g, g * hpg + np.arange(hpg), np.arange(hpg)] = 1.0
    head_sel = jnp.asarray(sel)

    def rows(colf):
        return lambda b, g, c: (rb0 + b * nc + c, colf(g))

    def cstate(width, colf):
        return pl.BlockSpec((None,) * ncl + (nb, kconv - 1, width), lambda b, g, c: tuple(conv_lead) + (b, 0, colf(g)))

    def wspec(nrows, width, colf):
        return pl.BlockSpec((None, nrows, width), lambda b, g, c: (layer, 0, colf(g)))

    xcol = lambda g: g
    bcol = lambda g: b_blk + g
    ccol = lambda g: c_blk + g
    in_specs = [
        pl.BlockSpec((TILE, gw), rows(xcol)), pl.BlockSpec((TILE, n_state), rows(bcol)),
        pl.BlockSpec((TILE, n_state), rows(ccol)), pl.BlockSpec((TILE, gw), rows(xcol)),
        pl.BlockSpec((TILE, LANE), rows(lambda g: 0)),
        cstate(gw, xcol), cstate(n_state, bcol), cstate(n_state, ccol),
        pl.BlockSpec((None,) * nhl + (nb, hpg, p_dim, n_state), lambda b, g, c: tuple(h0_lead) + (b, g, 0, 0)),
        wspec(kconv, gw, xcol), wspec(kconv, n_state, bcol), wspec(kconv, n_state, ccol),
        wspec(1, gw, xcol), wspec(1, n_state, bcol), wspec(1, n_state, ccol),
        wspec(1, LANE, xcol), wspec(1, LANE, xcol), wspec(1, gw, xcol), wspec(1, gw, xcol),
        pl.BlockSpec((None, LANE, LANE), lambda b, g, c: (g, 0, 0)),
        pl.BlockSpec(memory_space=pl.ANY), pl.BlockSpec(memory_space=pl.ANY),
    ]
    return pl.pallas_call(
        functools.partial(_ssd_body, q=q, nb=nb, hpg=hpg, kconv=kconv),
        out_shape=(jax.ShapeDtypeStruct(out_buf.shape, out_buf.dtype),
                   jax.ShapeDtypeStruct(h_buf.shape, h_buf.dtype)),
        grid=(n_outer, SSM_GROUPS, nc),
        in_specs=in_specs,
        out_specs=(pl.BlockSpec((TILE, gw), rows(xcol)),
                   pl.BlockSpec((None, nb, hpg, p_dim, n_state), lambda b, g, c: (layer, b, g, 0, 0))),
        scratch_shapes=[pltpu.VMEM((nb, gw, n_state), F32), pltpu.VMEM((nb, SUBLANE + q, gw), F32),
                        pltpu.VMEM((nb, SUBLANE + q, n_state), F32), pltpu.VMEM((nb, SUBLANE + q, n_state), F32)],
        input_output_aliases={20: 0, 21: 1},
        compiler_params=_cparams(("arbitrary", "arbitrary", "arbitrary")),
        name="ssd_mix",
    )(xbc, xbc, xbc, z, dtr, conv_state, conv_state, conv_state, h0, conv_w, conv_w, conv_w, conv_b, conv_b, conv_b,
      dtb, alog, dexp, nw, head_sel, out_buf, h_buf)


def _cmlp_body(u_ref, v_ref, ws_ref, bs_ref, g_ref, b_ref, alias_ref, *out_refs, q, n_heads, emit_v):
    del alias_ref
    o_ref = out_refs[0]
    vn = _layer_norm(v_ref[...], g_ref[...], b_ref[...])
    if emit_v:
        out_refs[1][...] = vn
    vb = vn.astype(BF16)
    row = lax.broadcasted_iota(I32, (TILE, TILE), 0)
    col = lax.broadcasted_iota(I32, (TILE, TILE), 1)
    causal = ((row // q) == (col // q)) & (col <= row)
    hd = vb.shape[1] // n_heads
    bs = bs_ref[...]
    for h in range(n_heads):
        sl = slice(h * hd, (h + 1) * hd)
        wm = jnp.where(causal, ws_ref[h], 0.0).astype(BF16)
        mixed = jnp.dot(wm, vb[:, sl], preferred_element_type=F32) + bs[:, h:h + 1]
        o_ref[:, sl] = (u_ref[:, sl].astype(F32) * mixed).astype(o_ref.dtype)


def _cmlp_mix(u, v, ws, ws_lead, bs_t, vg, vb, out_buf, row0, n_tiles, q, emit_v):
    t, cdim = u.shape
    n_heads = ws.shape[-3]
    rb0 = row0 // TILE
    nl = len(ws_lead)
    rows = lambda i: (rb0 + i, 0)
    fixed = lambda i: (0, 0)
    out_shape = [jax.ShapeDtypeStruct(out_buf.shape, out_buf.dtype)]
    out_specs = [pl.BlockSpec((TILE, cdim), rows)]
    if emit_v:
        out_shape.append(jax.ShapeDtypeStruct((n_tiles * TILE, cdim), F32))
        out_specs.append(pl.BlockSpec((TILE, cdim), lambda i: (i, 0)))
    return pl.pallas_call(
        functools.partial(_cmlp_body, q=q, n_heads=n_heads, emit_v=emit_v),
        out_shape=tuple(out_shape),
        grid=(n_tiles,),
        in_specs=[pl.BlockSpec((TILE, cdim), rows), pl.BlockSpec((TILE, cdim), rows),
                  pl.BlockSpec((None,) * nl + (n_heads, TILE, TILE), lambda i: tuple(ws_lead) + (0, 0, 0)),
                  pl.BlockSpec((TILE, LANE), fixed), pl.BlockSpec((1, cdim), fixed), pl.BlockSpec((1, cdim), fixed),
                  pl.BlockSpec(memory_space=pl.ANY)],
        out_specs=tuple(out_specs),
        input_output_aliases={6: 0},
        compiler_params=_cparams(("arbitrary",)),
        name="cmlp_mix",
    )(u, v, ws, bs_t, vg, vb, out_buf)


def _pick(n, target):
    t = min(n, target)
    while n % t:
        t -= SUBLANE
    return t


def kernel(x_prompt, x_sample, state_pool, state_conv, state_ssm, w_in_even, pool_w, pool_scale, conv_w, conv_b, dt_bias, a_log, d_skip, ssm_norm_w, w_out_even, cmlp_w_in, cmlp_b_in, cmlp_v_g, cmlp_v_b, cmlp_w_s, cmlp_b_s, cmlp_w_out, ln_mix_g, ln_mix_b, ln_ffn_g, ln_ffn_b, moe_w_group, moe_w_expert, moe_w_gate, moe_w_up, moe_w_down):
    bp, lp, d = x_prompt.shape
    bs, ls, _ = x_sample.shape
    depth = ln_mix_g.shape[0]
    alpha = float((2 * depth) ** 0.25)
    tp, ts = bp * lp, bs * ls
    t = tp + ts
    pool_dim = state_pool.shape[-1]
    conv_dim = state_conv.shape[-1]
    n_heads_ssm = state_ssm.shape[2]
    inner = n_heads_ssm * SSM_HEAD_DIM
    n_state = state_ssm.shape[-1]
    hpg = n_heads_ssm // SSM_GROUPS
    n_groups = moe_w_group.shape[-1]
    n_experts = moe_w_expert.shape[-1]
    per_group = n_experts // n_groups
    cdim = cmlp_w_out.shape[1]
    c_heads = cmlp_w_s.shape[1]
    assert n_state == LANE and SEQ_CHUNK == TILE and lp % TILE == 0 and TILE % ls == 0 and ts % TILE == 0
    assert n_groups + n_experts <= LANE and hpg <= LANE
    q_s = math.gcd(ls, SEQ_CHUNK)
    assert q_s == ls
    nb_s = TILE // q_s
    n_blocks = -(-(t * MOE_TOPK) // MOE_BLOCK) + n_experts

    tm = _pick(t, 1024)
    tm_ln = _pick(t, 256)
    tn = 512

    x = jnp.concatenate([x_prompt.reshape(tp, d), x_sample.reshape(ts, d)], axis=0)
    xb = x.astype(BF16)

    dt_col = pool_dim + inner + conv_dim

    def group_pad(v):
        lead = v.shape[:-1]
        v = v.reshape(lead + (SSM_GROUPS, hpg))
        v = jnp.pad(v, [(0, 0)] * len(lead) + [(0, 0), (0, LANE - hpg)])
        return v.reshape(lead + (SSM_GROUPS * LANE,))

    w_dt = jnp.pad(w_in_even[:, :, dt_col:], ((0, 0), (0, 0), (0, LANE - n_heads_ssm)))
    dtb_p = group_pad(dt_bias)[:, None, :]
    alog_p = group_pad(a_log)[:, None, :]
    dexp = jnp.repeat(d_skip, SSM_HEAD_DIM, axis=-1)[:, None, :]
    nw3 = ssm_norm_w[:, None, :]
    conv_b3 = conv_b[:, None, :]
    w_router = jnp.concatenate(
        [moe_w_group, moe_w_expert, jnp.zeros((depth, d, LANE - n_groups - n_experts), F32)], axis=-1).astype(BF16)
    zero_pool = jnp.zeros((bp, POOL_BUF, pool_dim), F32)
    zero_conv = jnp.zeros((bp, state_conv.shape[2], conv_dim), F32)
    n_even = state_ssm.shape[0]
    zero_ssm = jnp.zeros((bp,) + state_ssm.shape[2:], F32)
    ssm_p = jnp.zeros((n_even, bp) + state_ssm.shape[2:], F32)
    ssm_s = jnp.zeros(state_ssm.shape, F32)
    eye = jnp.eye(nb_s, dtype=F32)
    ws_s = jnp.einsum("ab,lhts->lhatbs", eye, cmlp_w_s[:, :, :q_s, :q_s]).reshape(-1, c_heads, TILE, TILE)
    bs_p = jnp.pad(jnp.swapaxes(cmlp_b_s, 1, 2), ((0, 0), (0, 0), (0, LANE - c_heads)))
    bs_s = jnp.tile(bs_p[:, :q_s], (1, nb_s, 1))

    def last_rows(a, row0, n_seq, seq_len, n):
        return jnp.stack([a[row0 + (s + 1) * seq_len - n:row0 + (s + 1) * seq_len] for s in range(n_seq)])

    def new_state(old, a, n):
        cur = a[tp:].reshape(bs, ls, a.shape[1])
        ext = cur if ls >= n else jnp.concatenate([old[:, ls:], cur], axis=1)
        return last_rows(a, 0, bp, lp, n), ext[:, ext.shape[1] - n:]

    pool_out, conv_out, v_out = [], [], []
    for layer in range(depth):
        i = layer // 2
        if layer % 2 == 0:
            lhs = [(xb, 0)]
            u_a = _matmul(lhs, w_in_even, (i,), (0,), d, 0, pool_dim, tn, tm, F32, name="in_pool")
            z = _matmul(lhs, w_in_even, (i,), (0,), d, pool_dim, inner, tn, tm, F32, name="in_gate")
            xbc = _matmul(lhs, w_in_even, (i,), (0,), d, pool_dim + inner, conv_dim, tn, tm, F32, name="in_conv")
            dtr = _matmul(lhs, w_dt, (i,), (0,), d, 0, LANE, LANE, tm, F32, name="in_dt")

            mixed = jnp.zeros((t, pool_dim), BF16)
            mixed = _pool_mix(u_a, zero_pool, (), mixed, 0, bp, lp // TILE, TILE, 1, 0)
            mixed = _pool_mix(u_a, state_pool, (i,), mixed, tp, ts // TILE, 1, q_s, nb_s, PAST_LEN)
            y_a = _group_matmul(mixed, pool_w, i, pool_scale[i][None, :], tm)

            y_b = jnp.zeros((t, inner), BF16)
            y_b, ssm_p = _ssd_mix(xbc, z, dtr, zero_conv, (), zero_ssm, (), conv_w, conv_b3, dtb_p, alog_p, dexp, nw3,
                                  i, y_b, ssm_p, 0, bp, lp // TILE, TILE, 1, inner)
            y_b, ssm_s = _ssd_mix(xbc, z, dtr, state_conv, (i,), state_ssm, (i,), conv_w, conv_b3, dtb_p, alog_p,
                                  dexp, nw3, i, y_b, ssm_s, tp, ts // TILE, 1, q_s, nb_s, inner)

            kc = pool_dim
            assert inner % kc == 0
            lhs_out = [(y_a, 0)] + [(y_b, k) for k in range(inner // kc)]
            mix = _matmul(lhs_out, w_out_even, (i,), tuple(range(1 + inner // kc)), kc, 0, d, tn, _pick(t, 256), F32,
                          name="out_even")

            pool_out.append(new_state(state_pool[i], u_a, POOL_BUF))
            conv_out.append(new_state(state_conv[i], xbc, state_conv.shape[2]))
        else:
            lhs = [(xb, 0)]
            bias = cmlp_b_in[i][None, :]
            u = _matmul(lhs, cmlp_w_in, (i,), (0,), d, 0, cdim, tn, tm, F32, bias=bias[:, :cdim], act="gelu",
                        name="cmlp_in_u")
            v = _matmul(lhs, cmlp_w_in, (i,), (0,), d, cdim, cdim, tn, tm, F32, bias=bias[:, cdim:], act="gelu",
                        name="cmlp_in_v")
            vg, vb_ = cmlp_v_g[i][None, :], cmlp_v_b[i][None, :]
            gated = jnp.zeros((t, cdim), BF16)
            (gated,) = _cmlp_mix(u, v, cmlp_w_s, (i,), bs_p[i], vg, vb_, gated, 0, tp // TILE, SEQ_CHUNK, False)
            gated, v_n = _cmlp_mix(u, v, ws_s, (i,), bs_s[i], vg, vb_, gated, tp, ts // TILE, q_s, True)
            v_out.append(v_n.reshape(bs, ls, cdim))
            mix = _matmul([(gated, 0)], cmlp_w_out, (i,), (0,), cdim, 0, d, tn, tm, F32, name="cmlp_out")

        x, xp, rt = _ln_router(x, mix, ln_mix_g[layer][None, :], ln_mix_b[layer][None, :], w_router[layer], alpha,
                               n_groups, per_group, tm_ln)
        dest, row_tok, plan = _moe_plan(rt, n_experts, n_blocks)
        xs = _dispatch(row_tok, plan[3], xp)
        hdim = moe_w_gate.shape[-1]
        h = _expert_up(plan, xs, moe_w_gate, moe_w_up, layer, min(hdim, 512))
        y_rows = _expert_down(plan, h, moe_w_down, layer, min(d, 2048))
        x, xb = _combine_ln(dest, x, rt, ln_ffn_g[layer][None, :], ln_ffn_b[layer][None, :], y_rows, alpha, tm_ln)

    y_prompt = x[:tp].reshape(bp, lp, d)
    y_sample = x[tp:].reshape(bs, ls, d)
    pool_p = jnp.stack([a for a, _ in pool_out])
    pool_s = jnp.stack([b for _, b in pool_out])
    conv_p = jnp.stack([a for a, _ in conv_out])
    conv_s = jnp.stack([b for _, b in conv_out])
    v_s = jnp.stack(v_out)
    return (y_prompt, y_sample, pool_p, pool_s, conv_p, conv_s, ssm_p, ssm_s, v_s)
```

```python
import functools
import math

import jax
import jax.numpy as jnp
from jax import lax
from jax.experimental import pallas as pl
from jax.experimental.pallas import tpu as pltpu

F32 = jnp.float32
BF16 = jnp.bfloat16
U32 = jnp.uint32
I32 = jnp.int32

POOL_WINDOWS = (2, 4, 8, 16)
POOL_BUF = max(POOL_WINDOWS) - 1
SSM_HEAD_DIM = 64
SSM_GROUPS = 8
SEQ_CHUNK = 128
MOE_TOPK = 2
PAST_LEN = 16384
LN_EPS = 1e-5
RMS_EPS = 1e-5

LANE = 128
SUBLANE = 8
V7X_VMEM_BYTES = 64 * 1024 * 1024
VMEM_LIMIT = V7X_VMEM_BYTES - 8 * 1024 * 1024
TILE = 128
MOE_BLOCK = 256

_NT = (((1,), (1,)), ((), ()))


def _cparams(sem):
    return pltpu.CompilerParams(dimension_semantics=sem, vmem_limit_bytes=VMEM_LIMIT)


def _sigmoid(x):
    return 0.5 * (jnp.tanh(0.5 * x) + 1.0)


def _silu(x):
    return x * _sigmoid(x)


def _gelu_tanh(x):
    return 0.5 * x * (1.0 + jnp.tanh(math.sqrt(2.0 / math.pi) * (x + 0.044715 * (x * x * x))))


def _softplus(x):
    return jnp.maximum(x, 0.0) + jnp.log1p(jnp.exp(-jnp.abs(x)))


def _layer_norm(y, g, b):
    mu = jnp.mean(y, axis=-1, keepdims=True)
    d = y - mu
    var = jnp.mean(d * d, axis=-1, keepdims=True)
    return d * lax.rsqrt(var + LN_EPS) * g + b


def _mm_body(*refs, n_chunks, has_bias, has_scale, act, w_t=False):
    xs = refs[:n_chunks]
    ws = refs[n_chunks:2 * n_chunks]
    pos = 2 * n_chunks
    bias_ref = scale_ref = None
    if has_bias:
        bias_ref = refs[pos]
        pos += 1
    if has_scale:
        scale_ref = refs[pos]
        pos += 1
    o_ref, wb = refs[pos], refs[pos + 1]

    @pl.when(pl.program_id(1) == 0)
    def _():
        for c in range(n_chunks):
            wb[c] = ws[c][...].astype(BF16)

    acc = None
    for c in range(n_chunks):
        if w_t:
            d = lax.dot_general(xs[c][...], wb[c], _NT, preferred_element_type=F32)
        else:
            d = jnp.dot(xs[c][...], wb[c], preferred_element_type=F32)
        acc = d if acc is None else acc + d
    if has_scale:
        acc = acc * scale_ref[...]
    if has_bias:
        acc = acc + bias_ref[...]
    if act == "gelu":
        acc = _gelu_tanh(acc)
    o_ref[...] = acc.astype(o_ref.dtype)


def _matmul(lhs, w, w_lead, w_row_blocks, kc, col0, n_cols, tn, tm, out_dtype, bias=None, act=None, name="mm",
            w_t=False):
    n_chunks = len(lhs)
    m = lhs[0][0].shape[0]
    assert m % tm == 0 and n_cols % tn == 0 and col0 % tn == 0
    cb0 = col0 // tn
    nl = len(w_lead)
    in_specs = []
    for _, cb in lhs:
        in_specs.append(pl.BlockSpec((tm, kc), functools.partial(lambda j, i, cb: (i, cb), cb=cb)))
    for rb in w_row_blocks:
        if w_t:
            in_specs.append(pl.BlockSpec((None,) * nl + (tn, kc),
                                         functools.partial(lambda j, i, rb: tuple(w_lead) + (cb0 + j, rb), rb=rb)))
        else:
            in_specs.append(pl.BlockSpec((None,) * nl + (kc, tn),
                                         functools.partial(lambda j, i, rb: tuple(w_lead) + (rb, cb0 + j), rb=rb)))
    args = [a for a, _ in lhs] + [w] * n_chunks
    if bias is not None:
        in_specs.append(pl.BlockSpec((1, tn), lambda j, i: (0, j)))
        args.append(bias)
    return pl.pallas_call(
        functools.partial(_mm_body, n_chunks=n_chunks, has_bias=bias is not None, has_scale=False, act=act, w_t=w_t),
        out_shape=jax.ShapeDtypeStruct((m, n_cols), out_dtype),
        grid=(n_cols // tn, m // tm),
        in_specs=in_specs,
        out_specs=pl.BlockSpec((tm, tn), lambda j, i: (i, j)),
        scratch_shapes=[pltpu.VMEM((n_chunks, tn, kc) if w_t else (n_chunks, kc, tn), BF16)],
        compiler_params=_cparams(("arbitrary", "arbitrary")),
        name=name,
    )(*args)


def _group_matmul(x, w, layer, scale, tm):
    m = x.shape[0]
    _, n_g, gd, _ = w.shape
    return pl.pallas_call(
        functools.partial(_mm_body, n_chunks=1, has_bias=False, has_scale=True, act=None),
        out_shape=jax.ShapeDtypeStruct((m, n_g * gd), BF16),
        grid=(n_g, m // tm),
        in_specs=[pl.BlockSpec((tm, gd), lambda g, i: (i, g)),
                  pl.BlockSpec((None, None, gd, gd), lambda g, i: (layer, g, 0, 0)),
                  pl.BlockSpec((1, gd), lambda g, i: (0, g))],
        out_specs=pl.BlockSpec((tm, gd), lambda g, i: (i, g)),
        scratch_shapes=[pltpu.VMEM((1, gd, gd), BF16)],
        compiler_params=_cparams(("arbitrary", "arbitrary")),
        name="pool_proj",
    )(x, w, scale)


def _pack_bf16_pair(y):
    half = y.shape[1] // 2
    hi = lax.bitcast_convert_type(y[:, :half].astype(BF16).astype(F32), U32)
    lo = lax.bitcast_convert_type(y[:, half:].astype(BF16).astype(F32), U32)
    return (hi & jnp.uint32(0xFFFF0000)) | (lo >> 16)


def _unpack_bf16_pair(p):
    hi = lax.bitcast_convert_type(p & jnp.uint32(0xFFFF0000), F32).astype(BF16)
    lo = lax.bitcast_convert_type(p << 16, F32).astype(BF16)
    return hi, lo


def _ln_router_body(x_ref, mix_ref, g_ref, b_ref, wr_ref, xo_ref, xp_ref, rt_ref, *, alpha, n_groups, per_group):
    y = _layer_norm(alpha * x_ref[...] + mix_ref[...], g_ref[...], b_ref[...])
    xo_ref[...] = y
    xp_ref[...] = _pack_bf16_pair(y)
    logits = jnp.dot(y.astype(BF16), wr_ref[...], preferred_element_type=F32)
    rows = logits.shape[0]
    lane = lax.broadcasted_iota(I32, (rows, LANE), 1).astype(F32)
    neg = -jnp.inf
    far = float(LANE)
    is_g = lane < n_groups
    gl = jnp.where(is_g, logits, neg)
    gmax = jnp.max(gl, axis=1, keepdims=True)
    gsel = jnp.min(jnp.where(gl == gmax, lane, far), axis=1, keepdims=True)
    gden = jnp.sum(jnp.where(is_g, jnp.exp(jnp.where(is_g, logits, gmax) - gmax), 0.0), axis=1, keepdims=True)
    gw = 1.0 / gden
    lo = n_groups + gsel * per_group
    el = jnp.where((lane >= lo) & (lane < lo + per_group), logits, neg)
    e1 = jnp.max(el, axis=1, keepdims=True)
    i1 = jnp.min(jnp.where(el == e1, lane, far), axis=1, keepdims=True)
    el2 = jnp.where(lane == i1, neg, el)
    e2 = jnp.max(el2, axis=1, keepdims=True)
    i2 = jnp.min(jnp.where(el2 == e2, lane, far), axis=1, keepdims=True)
    t = jnp.exp(e2 - e1)
    w1 = gw / (1.0 + t)
    w2 = gw * t / (1.0 + t)
    out = jnp.where(lane == 0, i1 - n_groups,
                    jnp.where(lane == 1, i2 - n_groups,
                              jnp.where(lane == 2, w1, jnp.where(lane == 3, w2, 0.0))))
    rt_ref[...] = out


def _ln_router(x, mix, g, b, w_router, alpha, n_groups, per_group, tm):
    t, d = x.shape
    row = lambda i: (i, 0)
    fixed = lambda i: (0, 0)
    return pl.pallas_call(
        functools.partial(_ln_router_body, alpha=alpha, n_groups=n_groups, per_group=per_group),
        out_shape=(jax.ShapeDtypeStruct((t, d), F32), jax.ShapeDtypeStruct((t, d // 2), U32),
                   jax.ShapeDtypeStruct((t, LANE), F32)),
        grid=(t // tm,),
        in_specs=[pl.BlockSpec((tm, d), row), pl.BlockSpec((tm, d), row), pl.BlockSpec((1, d), fixed),
                  pl.BlockSpec((1, d), fixed), pl.BlockSpec((d, LANE), fixed)],
        out_specs=(pl.BlockSpec((tm, d), row), pl.BlockSpec((tm, d // 2), row), pl.BlockSpec((tm, LANE), row)),
        compiler_params=_cparams(("arbitrary",)),
        name="ln_router",
    )(x, mix, g, b, w_router)


def _row_gather(n, src_hbm, src_row, dst, sem, wait):
    def body(g, carry):
        for u in range(SUBLANE):
            cp = pltpu.make_async_copy(src_hbm.at[pl.ds(src_row(g * SUBLANE + u), 1)], dst.at[g, pl.ds(u, 1)], sem)
            if wait:
                cp.wait()
            else:
                cp.start(priority=u % 2)
        return carry

    lax.fori_loop(0, n // SUBLANE, body, 0)


def _dispatch_body(tok_ref, meta_ref, xp_hbm, o_ref, buf, sems):
    b = pl.program_id(0)
    n_used = meta_ref[0]

    def rows(blk, wait):
        slot = blk % 2
        _row_gather(MOE_BLOCK, xp_hbm, lambda r: tok_ref[blk * MOE_BLOCK + r], buf.at[slot], sems.at[slot], wait)

    @pl.when((b == 0) & (n_used > 0))
    def _():
        rows(b, False)

    @pl.when(b + 1 < n_used)
    def _():
        rows(b + 1, False)

    @pl.when(b < n_used)
    def _():
        rows(b, True)
        o_ref[...] = buf[b % 2].reshape(o_ref.shape)

    @pl.when(b >= n_used)
    def _():
        o_ref[...] = jnp.zeros_like(o_ref)


def _dispatch(row_tok, meta, xp):
    dh = xp.shape[1]
    n_rows = row_tok.shape[0]
    return pl.pallas_call(
        _dispatch_body,
        out_shape=jax.ShapeDtypeStruct((n_rows, dh), U32),
        grid_spec=pltpu.PrefetchScalarGridSpec(
            num_scalar_prefetch=2,
            grid=(n_rows // MOE_BLOCK,),
            in_specs=[pl.BlockSpec(memory_space=pl.ANY)],
            out_specs=pl.BlockSpec((MOE_BLOCK, dh), lambda b, tok, meta: (b, 0)),
            scratch_shapes=[pltpu.VMEM((2, MOE_BLOCK // SUBLANE, SUBLANE, dh), U32), pltpu.SemaphoreType.DMA((2,))],
        ),
        compiler_params=_cparams(("arbitrary",)),
        name="moe_dispatch",
    )(row_tok, meta, xp)


def _weight_jobs(first_ref, rank_ref, dist_ref, meta_ref, w_hbms, layer, chunk, wbuf, sems):
    p, b = pl.program_id(0), pl.program_id(1)
    n_phase = pl.num_programs(0)
    nd = meta_ref[1]
    r = rank_ref[b]
    is_first = first_ref[b] == 1
    slot = (p * nd + r) & 1

    def copies(phase, rnk, slot_):
        e = dist_ref[rnk]
        col = pl.multiple_of(phase * chunk, chunk)
        return [pltpu.make_async_copy(w.at[layer, e, :, pl.ds(col, chunk)], wbuf.at[slot_, i], sems.at[slot_])
                for i, w in enumerate(w_hbms)]

    @pl.when(is_first & (p == 0) & (r == 0))
    def _():
        for cp in copies(p, r, slot):
            cp.start()

    wrap = r + 1 >= nd
    nxt_p = jnp.where(wrap, p + 1, p)
    nxt_r = jnp.where(wrap, 0, r + 1)

    @pl.when(is_first & (nxt_p < n_phase))
    def _():
        for cp in copies(nxt_p, nxt_r, 1 - slot):
            cp.start()

    @pl.when(is_first)
    def _():
        for cp in copies(p, r, slot):
            cp.wait()

    return is_first, slot


def _expert_up_body(first_ref, rank_ref, dist_ref, meta_ref, xs_ref, wg_hbm, wu_hbm, h_ref, wbuf, wgb, wub, sems,
                    *, layer, hc):
    is_first, slot = _weight_jobs(first_ref, rank_ref, dist_ref, meta_ref, (wg_hbm, wu_hbm), layer, hc, wbuf, sems)
    blk = pl.program_id(1)
    n_used = meta_ref[0]

    @pl.when(is_first)
    def _():
        wgb[...] = wbuf[slot, 0].astype(BF16)
        wub[...] = wbuf[slot, 1].astype(BF16)

    @pl.when(blk < n_used)
    def _():
        hi, lo = _unpack_bf16_pair(xs_ref[...])
        half = hi.shape[1]
        gate = (jnp.dot(hi, wgb[:half], preferred_element_type=F32)
                + jnp.dot(lo, wgb[half:], preferred_element_type=F32))
        up = (jnp.dot(hi, wub[:half], preferred_element_type=F32)
              + jnp.dot(lo, wub[half:], preferred_element_type=F32))
        h_ref[...] = (_silu(gate) * up).astype(BF16)

    @pl.when(blk >= n_used)
    def _():
        h_ref[...] = jnp.zeros_like(h_ref)


def _expert_up(plan, xs, w_gate, w_up, layer, hc):
    n_rows, dh = xs.shape
    d, hdim = w_gate.shape[2], w_gate.shape[3]
    n_blk = n_rows // MOE_BLOCK
    return pl.pallas_call(
        functools.partial(_expert_up_body, layer=layer, hc=hc),
        out_shape=jax.ShapeDtypeStruct((n_rows, hdim), BF16),
        grid_spec=pltpu.PrefetchScalarGridSpec(
            num_scalar_prefetch=4,
            grid=(hdim // hc, n_blk),
            in_specs=[pl.BlockSpec((MOE_BLOCK, dh), lambda p, b, *_: (b, 0)),
                      pl.BlockSpec(memory_space=pl.ANY), pl.BlockSpec(memory_space=pl.ANY)],
            out_specs=pl.BlockSpec((MOE_BLOCK, hc), lambda p, b, *_: (b, p)),
            scratch_shapes=[pltpu.VMEM((2, 2, d, hc), F32), pltpu.VMEM((d, hc), BF16), pltpu.VMEM((d, hc), BF16),
                            pltpu.SemaphoreType.DMA((2,))],
        ),
        compiler_params=_cparams(("arbitrary", "arbitrary")),
        name="expert_up",
    )(*plan, xs, w_gate, w_up)


def _expert_down_body(first_ref, rank_ref, dist_ref, meta_ref, h_ref, wd_hbm, y_ref, wbuf, wdb, sems, *, layer, oc):
    is_first, slot = _weight_jobs(first_ref, rank_ref, dist_ref, meta_ref, (wd_hbm,), layer, oc, wbuf, sems)
    blk = pl.program_id(1)
    n_used = meta_ref[0]

    @pl.when(is_first)
    def _():
        wdb[...] = wbuf[slot, 0].astype(BF16)

    @pl.when(blk < n_used)
    def _():
        y_ref[...] = jnp.dot(h_ref[...], wdb[...], preferred_element_type=F32)

    @pl.when(blk >= n_used)
    def _():
        y_ref[...] = jnp.zeros_like(y_ref)


def _expert_down(plan, h, w_down, layer, oc):
    n_rows, hdim = h.shape
    d = w_down.shape[3]
    n_blk = n_rows // MOE_BLOCK
    return pl.pallas_call(
        functools.partial(_expert_down_body, layer=layer, oc=oc),
        out_shape=jax.ShapeDtypeStruct((n_rows, d), F32),
        grid_spec=pltpu.PrefetchScalarGridSpec(
            num_scalar_prefetch=4,
            grid=(d // oc, n_blk),
            in_specs=[pl.BlockSpec((MOE_BLOCK, hdim), lambda q, b, *_: (b, 0)), pl.BlockSpec(memory_space=pl.ANY)],
            out_specs=pl.BlockSpec((MOE_BLOCK, oc), lambda q, b, *_: (b, q)),
            scratch_shapes=[pltpu.VMEM((2, 1, hdim, oc), F32), pltpu.VMEM((hdim, oc), BF16),
                            pltpu.SemaphoreType.DMA((2,))],
        ),
        compiler_params=_cparams(("arbitrary", "arbitrary")),
        name="expert_down",
    )(*plan, h, w_down)


def _combine_ln_body(dest_ref, x_ref, rt_ref, g_ref, b_ref, y_hbm, xo_ref, xb_ref, rows, sems, *, alpha, tm):
    s = pl.program_id(0)

    def gather(step, wait):
        slot = step % 2
        for k in range(MOE_TOPK):
            _row_gather(tm, y_hbm, lambda r: dest_ref[(step * tm + r) * MOE_TOPK + k], rows.at[slot, k],
                        sems.at[slot], wait)

    @pl.when(s == 0)
    def _():
        gather(s, False)

    @pl.when(s + 1 < pl.num_programs(0))
    def _():
        gather(s + 1, False)

    gather(s, True)
    slot = s % 2
    rt = rt_ref[...]
    x = x_ref[...]
    ffn = rt[:, 2:3] * rows[slot, 0].reshape(x.shape) + rt[:, 3:4] * rows[slot, 1].reshape(x.shape)
    y = _layer_norm(alpha * x + ffn, g_ref[...], b_ref[...])
    xo_ref[...] = y
    xb_ref[...] = y.astype(BF16)


def _combine_ln(dest, x, rt, g, b, y_rows, alpha, tm):
    t, d = x.shape
    row = lambda i, de: (i, 0)
    fixed = lambda i, de: (0, 0)
    return pl.pallas_call(
        functools.partial(_combine_ln_body, alpha=alpha, tm=tm),
        out_shape=(jax.ShapeDtypeStruct((t, d), F32), jax.ShapeDtypeStruct((t, d), BF16)),
        grid_spec=pltpu.PrefetchScalarGridSpec(
            num_scalar_prefetch=1,
            grid=(t // tm,),
            in_specs=[pl.BlockSpec((tm, d), row), pl.BlockSpec((tm, LANE), row), pl.BlockSpec((1, d), fixed),
                      pl.BlockSpec((1, d), fixed), pl.BlockSpec(memory_space=pl.ANY)],
            out_specs=(pl.BlockSpec((tm, d), row), pl.BlockSpec((tm, d), row)),
            scratch_shapes=[pltpu.VMEM((2, MOE_TOPK, tm // SUBLANE, SUBLANE, d), F32), pltpu.SemaphoreType.DMA((2,))],
        ),
        compiler_params=_cparams(("arbitrary",)),
        name="moe_combine_ln",
    )(dest, x, rt, g, b, y_rows)


def _moe_plan(rt, n_experts, n_blocks):
    n_assign = rt.shape[0] * MOE_TOPK
    e = rt[:, :MOE_TOPK].astype(I32).reshape(-1)
    onehot = (e[:, None] == jnp.arange(n_experts, dtype=I32)[None, :]).astype(I32)
    csum = jnp.cumsum(onehot, axis=0)
    rank_in_e = jnp.sum(csum * onehot, axis=1) - 1
    counts = csum[-1]
    padded = (counts + MOE_BLOCK - 1) // MOE_BLOCK * MOE_BLOCK
    pad_end = jnp.cumsum(padded)
    pad_start = pad_end - padded
    dest = (jnp.sum(onehot * pad_start[None, :], axis=1) + rank_in_e).astype(I32)
    row_tok = jnp.zeros((n_blocks * MOE_BLOCK,), I32).at[dest].set(jnp.arange(n_assign, dtype=I32) // MOE_TOPK)
    blk = jnp.arange(n_blocks, dtype=I32)
    blk_e = jnp.minimum(jnp.searchsorted(pad_end, blk * MOE_BLOCK, side="right"), n_experts - 1).astype(I32)
    n_used = (pad_end[-1] // MOE_BLOCK).astype(I32)
    prev_e = jnp.concatenate([jnp.full((1,), -1, I32), blk_e[:-1]])
    first = ((blk < n_used) & (blk_e != prev_e)).astype(I32)
    rank = jnp.maximum(jnp.cumsum(first) - 1, 0).astype(I32)
    dist_e = jnp.argsort(counts == 0, stable=True).astype(I32)
    meta = jnp.stack([n_used, jnp.sum(first).astype(I32)])
    return dest, row_tok, (first, rank, dist_e, meta)


def _pool_body(u_ref, st_ref, alias_ref, o_ref, ext, *, q, nb, start_pos, gd):
    del alias_ref
    c = pl.program_id(1)
    head = POOL_BUF + 1

    @pl.when(c == 0)
    def _():
        for i in range(nb):
            ext[i, 1:head] = st_ref[i]

    pos = (start_pos + c * q + lax.broadcasted_iota(I32, (q, 1), 0) + 1).astype(F32)
    for i in range(nb):
        ext[i, head:head + q] = u_ref[i * q:(i + 1) * q]
    for i in range(nb):
        for gi, w in enumerate(POOL_WINDOWS):
            sl = slice(gi * gd, (gi + 1) * gd)
            acc = ext[i, head:head + q, sl]
            cur = acc
            for k in range(1, w):
                acc = acc + ext[i, head - k:head - k + q, sl]
            pooled = acc / jnp.minimum(pos, float(w))
            o_ref[i * q:(i + 1) * q, sl] = (pooled - cur).astype(o_ref.dtype)
    for i in range(nb):
        ext[i, 0:head] = ext[i, q:q + head]


def _pool_mix(u, state, lead, out_buf, row0, n_tiles_outer, nc, q, nb, start_pos):
    t, cdim = u.shape
    gd = cdim // len(POOL_WINDOWS)
    rb0 = row0 // TILE
    nl = len(lead)
    rows = lambda b, c: (rb0 + b * nc + c, 0)
    return pl.pallas_call(
        functools.partial(_pool_body, q=q, nb=nb, start_pos=start_pos, gd=gd),
        out_shape=jax.ShapeDtypeStruct(out_buf.shape, out_buf.dtype),
        grid=(n_tiles_outer, nc),
        in_specs=[pl.BlockSpec((TILE, cdim), rows),
                  pl.BlockSpec((None,) * nl + (nb, POOL_BUF, cdim), lambda b, c: tuple(lead) + (b, 0, 0)),
                  pl.BlockSpec(memory_space=pl.ANY)],
        out_specs=pl.BlockSpec((TILE, cdim), rows),
        scratch_shapes=[pltpu.VMEM((nb, POOL_BUF + 1 + q, cdim), F32)],
        input_output_aliases={2: 0},
        compiler_params=_cparams(("arbitrary", "arbitrary")),
        name="pool_mix",
    )(u, state, out_buf)


def _ssd_body(xs_ref, bm_ref, cm_ref, z_ref, dt_ref, sx_ref, sb_ref, sc_ref, h0_ref,
              wx_ref, wb_ref, wc_ref, bx_ref, bb_ref, bc_ref, dtb_ref, alog_ref, dexp_ref, nw_ref, alias_ref,
              halias_ref, yb_ref, hout_ref, hs, ex, eb, ec, *, q, nb, hpg, kconv):
    del alias_ref, halias_ref
    c = pl.program_id(2)
    p_dim = SSM_HEAD_DIM
    off = SUBLANE - (kconv - 1)

    @pl.when(c == 0)
    def _():
        hs[...] = h0_ref[...].reshape(hs.shape)
        for i in range(nb):
            ex[i, off:SUBLANE] = sx_ref[i]
            eb[i, off:SUBLANE] = sb_ref[i]
            ec[i, off:SUBLANE] = sc_ref[i]

    def conv(raw_ref, e_ref, w_ref, bias_ref):
        outs = []
        for i in range(nb):
            e_ref[i, SUBLANE:SUBLANE + q] = raw_ref[i * q:(i + 1) * q]
            acc = bias_ref[...] + e_ref[i, off:off + q] * w_ref[0:1, :]
            for k in range(1, kconv):
                acc = acc + e_ref[i, off + k:off + k + q] * w_ref[k:k + 1, :]
            outs.append(acc)
            e_ref[i, 0:SUBLANE] = e_ref[i, q:q + SUBLANE]
        o = outs[0] if nb == 1 else jnp.concatenate(outs, axis=0)
        return _silu(o)

    xs = conv(xs_ref, ex, wx_ref, bx_ref)
    bm = conv(bm_ref, eb, wb_ref, bb_ref)
    cm = conv(cm_ref, ec, wc_ref, bc_ref)
    dt = _softplus(dt_ref[...] + dtb_ref[...])
    dta = dt * (-jnp.exp(alog_ref[...]))
    row = lax.broadcasted_iota(I32, (TILE, TILE), 0)
    col = lax.broadcasted_iota(I32, (TILE, TILE), 1)
    same = (row // q) == (col // q)
    causal = same & (col <= row)
    hp = lax.Precision.HIGHEST
    acs = jnp.dot(causal.astype(F32), dta, preferred_element_type=F32, precision=hp)
    tot = jnp.dot(same.astype(F32), dta, preferred_element_type=F32, precision=hp)
    acs_t = acs.T
    tot_t = tot.T
    e_acs = jnp.exp(acs)
    to_end = jnp.exp(tot - acs)
    bm_b = bm.astype(BF16)
    cm_b = cm.astype(BF16)
    scores = lax.dot_general(cm_b, bm_b, _NT, preferred_element_type=F32)
    lo_half = lax.broadcasted_iota(I32, (TILE, LANE), 1) < p_dim

    y_parts, xw_t_parts, ea_parts = [], [], []
    for p in range(hpg // 2):
        j0, j1 = 2 * p, 2 * p + 1

        def expand(v):
            return jnp.where(lo_half, v[:, j0:j0 + 1], v[:, j1:j1 + 1])

        xs_p = xs[:, p * LANE:(p + 1) * LANE]
        xdt = xs_p * expand(dt)
        xdt_b = xdt.astype(BF16)
        ys = []
        for j in (j0, j1):
            decay = jnp.exp(jnp.where(causal, acs[:, j:j + 1] - acs_t[j:j + 1, :], -jnp.inf))
            ys.append(jnp.dot((scores * decay).astype(BF16), xdt_b, preferred_element_type=F32))
        y_parts.append(jnp.where(lo_half, ys[0], ys[1]))
        xw_t_parts.append((xdt * expand(to_end)).T.astype(BF16))
        ea_parts.append(expand(e_acs))
    y_diag = jnp.concatenate(y_parts, axis=1)
    xw_t = jnp.concatenate(xw_t_parts, axis=0)
    ea = jnp.concatenate(ea_parts, axis=1)

    sub = lax.broadcasted_iota(I32, (TILE, 1), 0) // q
    y_off = None
    for i in range(nb):
        h_i = hs[i]
        full = lax.dot_general(cm_b, h_i.astype(BF16), _NT, preferred_element_type=F32)
        y_off = full if nb == 1 else (jnp.where(sub == i, full, 0.0) + (0.0 if y_off is None else y_off))
        bm_i = bm_b if nb == 1 else jnp.where(sub == i, bm, 0.0).astype(BF16)
        st = jnp.dot(xw_t, bm_i, preferred_element_type=F32)
        for j in range(hpg):
            rs = slice(j * p_dim, (j + 1) * p_dim)
            cd = jnp.exp(tot_t[j:j + 1, i * q:i * q + 1])
            hs[i, rs, :] = h_i[rs, :] * cd + st[rs, :]

    y = y_diag + y_off * ea + xs * dexp_ref[...]
    y = y * _silu(z_ref[...])
    ms = jnp.mean(y * y, axis=1, keepdims=True)
    yb_ref[...] = (y * lax.rsqrt(ms + RMS_EPS) * nw_ref[...]).astype(yb_ref.dtype)

    @pl.when(c == pl.num_programs(2) - 1)
    def _():
        hout_ref[...] = hs[...].reshape(hout_ref.shape)


def _ssd_mix(xbc, z, dtr, conv_state, conv_lead, h0, h0_lead, conv_w, conv_b, dtb, alog, dexp, nw, layer,
             out_buf, h_buf, row0, n_outer, nc, q, nb, inner):
    n_state = LANE
    p_dim = SSM_HEAD_DIM
    gw = inner // SSM_GROUPS
    hpg = gw // SSM_HEAD_DIM
    assert hpg % 2 == 0 and gw % LANE == 0
    kconv = conv_w.shape[1]
    rb0 = row0 // TILE
    b_blk = inner // n_state
    c_blk = (inner + SSM_GROUPS * n_state) // n_state
    ncl, nhl = len(conv_lead), len(h0_lead)

    def rows(colf):
        return lambda b, g, c: (rb0 + b * nc + c, colf(g))

    def cstate(width, colf):
        return pl.BlockSpec((None,) * ncl + (nb, kconv - 1, width), lambda b, g, c: tuple(conv_lead) + (b, 0, colf(g)))

    def wspec(nrows, width, colf):
        return pl.BlockSpec((None, nrows, width), lambda b, g, c: (layer, 0, colf(g)))

    xcol = lambda g: g
    bcol = lambda g: b_blk + g
    ccol = lambda g: c_blk + g
    in_specs = [
        pl.BlockSpec((TILE, gw), rows(xcol)), pl.BlockSpec((TILE, n_state), rows(bcol)),
        pl.BlockSpec((TILE, n_state), rows(ccol)), pl.BlockSpec((TILE, gw), rows(xcol)),
        pl.BlockSpec((TILE, LANE), rows(xcol)),
        cstate(gw, xcol), cstate(n_state, bcol), cstate(n_state, ccol),
        pl.BlockSpec((None,) * nhl + (nb, hpg, p_dim, n_state), lambda b, g, c: tuple(h0_lead) + (b, g, 0, 0)),
        wspec(kconv, gw, xcol), wspec(kconv, n_state, bcol), wspec(kconv, n_state, ccol),
        wspec(1, gw, xcol), wspec(1, n_state, bcol), wspec(1, n_state, ccol),
        wspec(1, LANE, xcol), wspec(1, LANE, xcol), wspec(1, gw, xcol), wspec(1, gw, xcol),
        pl.BlockSpec(memory_space=pl.ANY), pl.BlockSpec(memory_space=pl.ANY),
    ]
    return pl.pallas_call(
        functools.partial(_ssd_body, q=q, nb=nb, hpg=hpg, kconv=kconv),
        out_shape=(jax.ShapeDtypeStruct(out_buf.shape, out_buf.dtype),
                   jax.ShapeDtypeStruct(h_buf.shape, h_buf.dtype)),
        grid=(n_outer, SSM_GROUPS, nc),
        in_specs=in_specs,
        out_specs=(pl.BlockSpec((TILE, gw), rows(xcol)),
                   pl.BlockSpec((None, nb, hpg, p_dim, n_state), lambda b, g, c: (layer, b, g, 0, 0))),
        scratch_shapes=[pltpu.VMEM((nb, gw, n_state), F32), pltpu.VMEM((nb, SUBLANE + q, gw), F32),
                        pltpu.VMEM((nb, SUBLANE + q, n_state), F32), pltpu.VMEM((nb, SUBLANE + q, n_state), F32)],
        input_output_aliases={19: 0, 20: 1},
        compiler_params=_cparams(("arbitrary", "arbitrary", "arbitrary")),
        name="ssd_mix",
    )(xbc, xbc, xbc, z, dtr, conv_state, conv_state, conv_state, h0, conv_w, conv_w, conv_w, conv_b, conv_b, conv_b,
      dtb, alog, dexp, nw, out_buf, h_buf)


def _cmlp_body(u_ref, v_ref, ws_ref, bs_ref, g_ref, b_ref, alias_ref, *out_refs, q, n_heads, emit_v):
    del alias_ref
    o_ref = out_refs[0]
    vn = _layer_norm(v_ref[...], g_ref[...], b_ref[...])
    if emit_v:
        out_refs[1][...] = vn
    vb = vn.astype(BF16)
    row = lax.broadcasted_iota(I32, (TILE, TILE), 0)
    col = lax.broadcasted_iota(I32, (TILE, TILE), 1)
    causal = ((row // q) == (col // q)) & (col <= row)
    hd = vb.shape[1] // n_heads
    bs = bs_ref[...]
    for h in range(n_heads):
        sl = slice(h * hd, (h + 1) * hd)
        wm = jnp.where(causal, ws_ref[h], 0.0).astype(BF16)
        mixed = jnp.dot(wm, vb[:, sl], preferred_element_type=F32) + bs[:, h:h + 1]
        o_ref[:, sl] = (u_ref[:, sl].astype(F32) * mixed).astype(o_ref.dtype)


def _cmlp_mix(u, v, ws, ws_lead, bs_t, vg, vb, out_buf, row0, n_tiles, q, emit_v):
    t, cdim = u.shape
    n_heads = ws.shape[-3]
    rb0 = row0 // TILE
    nl = len(ws_lead)
    rows = lambda i: (rb0 + i, 0)
    fixed = lambda i: (0, 0)
    out_shape = [jax.ShapeDtypeStruct(out_buf.shape, out_buf.dtype)]
    out_specs = [pl.BlockSpec((TILE, cdim), rows)]
    if emit_v:
        out_shape.append(jax.ShapeDtypeStruct((n_tiles * TILE, cdim), F32))
        out_specs.append(pl.BlockSpec((TILE, cdim), lambda i: (i, 0)))
    return pl.pallas_call(
        functools.partial(_cmlp_body, q=q, n_heads=n_heads, emit_v=emit_v),
        out_shape=tuple(out_shape),
        grid=(n_tiles,),
        in_specs=[pl.BlockSpec((TILE, cdim), rows), pl.BlockSpec((TILE, cdim), rows),
                  pl.BlockSpec((None,) * nl + (n_heads, TILE, TILE), lambda i: tuple(ws_lead) + (0, 0, 0)),
                  pl.BlockSpec((TILE, LANE), fixed), pl.BlockSpec((1, cdim), fixed), pl.BlockSpec((1, cdim), fixed),
                  pl.BlockSpec(memory_space=pl.ANY)],
        out_specs=tuple(out_specs),
        input_output_aliases={6: 0},
        compiler_params=_cparams(("arbitrary",)),
        name="cmlp_mix",
    )(u, v, ws, bs_t, vg, vb, out_buf)


def _pick(n, target):
    t = min(n, target)
    while n % t:
        t -= SUBLANE
    return t


def kernel(x_prompt, x_sample, state_pool, state_conv, state_ssm, w_in_even, pool_w, pool_scale, conv_w, conv_b, dt_bias, a_log, d_skip, ssm_norm_w, w_out_even, cmlp_w_in, cmlp_b_in, cmlp_v_g, cmlp_v_b, cmlp_w_s, cmlp_b_s, cmlp_w_out, ln_mix_g, ln_mix_b, ln_ffn_g, ln_ffn_b, moe_w_group, moe_w_expert, moe_w_gate, moe_w_up, moe_w_down):
    bp, lp, d = x_prompt.shape
    bs, ls, _ = x_sample.shape
    depth = ln_mix_g.shape[0]
    alpha = float((2 * depth) ** 0.25)
    tp, ts = bp * lp, bs * ls
    t = tp + ts
    pool_dim = state_pool.shape[-1]
    conv_dim = state_conv.shape[-1]
    n_heads_ssm = state_ssm.shape[2]
    inner = n_heads_ssm * SSM_HEAD_DIM
    n_state = state_ssm.shape[-1]
    hpg = n_heads_ssm // SSM_GROUPS
    n_groups = moe_w_group.shape[-1]
    n_experts = moe_w_expert.shape[-1]
    per_group = n_experts // n_groups
    cdim = cmlp_w_out.shape[1]
    c_heads = cmlp_w_s.shape[1]
    assert n_state == LANE and SEQ_CHUNK == TILE and lp % TILE == 0 and TILE % ls == 0 and ts % TILE == 0
    assert n_groups + n_experts <= LANE and hpg <= LANE
    q_s = math.gcd(ls, SEQ_CHUNK)
    assert q_s == ls
    nb_s = TILE // q_s
    n_blocks = -(-(t * MOE_TOPK) // MOE_BLOCK) + n_experts

    tm = _pick(t, 1024)
    tm_ln = _pick(t, 256)
    tn = 512

    x = jnp.concatenate([x_prompt.reshape(tp, d), x_sample.reshape(ts, d)], axis=0)
    xb = x.astype(BF16)

    dt_col = pool_dim + inner + conv_dim

    def group_pad(v):
        lead = v.shape[:-1]
        v = v.reshape(lead + (SSM_GROUPS, hpg))
        v = jnp.pad(v, [(0, 0)] * len(lead) + [(0, 0), (0, LANE - hpg)])
        return v.reshape(lead + (SSM_GROUPS * LANE,))

    w_in_t = jnp.swapaxes(w_in_even, 1, 2)
    w_dt_t = w_in_t[:, dt_col:].reshape(-1, SSM_GROUPS, hpg, d)
    w_dt_t = jnp.pad(w_dt_t, ((0, 0), (0, 0), (0, LANE - hpg), (0, 0))).reshape(-1, SSM_GROUPS * LANE, d)
    dtb_p = group_pad(dt_bias)[:, None, :]
    alog_p = group_pad(a_log)[:, None, :]
    dexp = jnp.repeat(d_skip, SSM_HEAD_DIM, axis=-1)[:, None, :]
    nw3 = ssm_norm_w[:, None, :]
    conv_b3 = conv_b[:, None, :]
    w_router = jnp.concatenate(
        [moe_w_group, moe_w_expert, jnp.zeros((depth, d, LANE - n_groups - n_experts), F32)], axis=-1).astype(BF16)
    zero_pool = jnp.zeros((bp, POOL_BUF, pool_dim), F32)
    zero_conv = jnp.zeros((bp, state_conv.shape[2], conv_dim), F32)
    n_even = state_ssm.shape[0]
    zero_ssm = jnp.zeros((bp,) + state_ssm.shape[2:], F32)
    ssm_p = jnp.zeros((n_even, bp) + state_ssm.shape[2:], F32)
    ssm_s = jnp.zeros(state_ssm.shape, F32)
    eye = jnp.eye(nb_s, dtype=F32)
    ws_s = jnp.einsum("ab,lhts->lhatbs", eye, cmlp_w_s[:, :, :q_s, :q_s]).reshape(-1, c_heads, TILE, TILE)
    bs_p = jnp.pad(jnp.swapaxes(cmlp_b_s, 1, 2), ((0, 0), (0, 0), (0, LANE - c_heads)))
    bs_s = jnp.tile(bs_p[:, :q_s], (1, nb_s, 1))

    def last_rows(a, row0, n_seq, seq_len, n):
        return jnp.stack([a[row0 + (s + 1) * seq_len - n:row0 + (s + 1) * seq_len] for s in range(n_seq)])

    def new_state(old, a, n):
        cur = a[tp:].reshape(bs, ls, a.shape[1])
        ext = cur if ls >= n else jnp.concatenate([old[:, ls:], cur], axis=1)
        return last_rows(a, 0, bp, lp, n), ext[:, ext.shape[1] - n:]

    pool_out, conv_out, v_out = [], [], []
    for layer in range(depth):
        i = layer // 2
        if layer % 2 == 0:
            lhs = [(xb, 0)]
            u_a = _matmul(lhs, w_in_t, (i,), (0,), d, 0, pool_dim, tn, tm, F32, name="in_pool", w_t=True)
            z = _matmul(lhs, w_in_t, (i,), (0,), d, pool_dim, inner, tn, tm, F32, name="in_gate", w_t=True)
            xbc = _matmul(lhs, w_in_t, (i,), (0,), d, pool_dim + inner, conv_dim, tn, tm, F32, name="in_conv",
                          w_t=True)
            dtr = _matmul(lhs, w_dt_t, (i,), (0,), d, 0, SSM_GROUPS * LANE, tn, tm, F32, name="in_dt", w_t=True)

            mixed = jnp.zeros((t, pool_dim), BF16)
            mixed = _pool_mix(u_a, zero_pool, (), mixed, 0, bp, lp // TILE, TILE, 1, 0)
            mixed = _pool_mix(u_a, state_pool, (i,), mixed, tp, ts // TILE, 1, q_s, nb_s, PAST_LEN)
            y_a = _group_matmul(mixed, pool_w, i, pool_scale[i][None, :], tm)

            y_b = jnp.zeros((t, inner), BF16)
            y_b, ssm_p = _ssd_mix(xbc, z, dtr, zero_conv, (), zero_ssm, (), conv_w, conv_b3, dtb_p, alog_p, dexp, nw3,
                                  i, y_b, ssm_p, 0, bp, lp // TILE, TILE, 1, inner)
            y_b, ssm_s = _ssd_mix(xbc, z, dtr, state_conv, (i,), state_ssm, (i,), conv_w, conv_b3, dtb_p, alog_p,
                                  dexp, nw3, i, y_b, ssm_s, tp, ts // TILE, 1, q_s, nb_s, inner)

            kc = pool_dim
            assert inner % kc == 0
            lhs_out = [(y_a, 0)] + [(y_b, k) for k in range(inner // kc)]
            mix = _matmul(lhs_out, w_out_even, (i,), tuple(range(1 + inner // kc)), kc, 0, d, tn, _pick(t, 256), F32,
                          name="out_even")

            pool_out.append(new_state(state_pool[i], u_a, POOL_BUF))
            conv_out.append(new_state(state_conv[i], xbc, state_conv.shape[2]))
        else:
            lhs = [(xb, 0)]
            bias = cmlp_b_in[i][None, :]
            u = _matmul(lhs, cmlp_w_in, (i,), (0,), d, 0, cdim, tn, tm, F32, bias=bias[:, :cdim], act="gelu",
                        name="cmlp_in_u")
            v = _matmul(lhs, cmlp_w_in, (i,), (0,), d, cdim, cdim, tn, tm, F32, bias=bias[:, cdim:], act="gelu",
                        name="cmlp_in_v")
            vg, vb_ = cmlp_v_g[i][None, :], cmlp_v_b[i][None, :]
            gated = jnp.zeros((t, cdim), BF16)
            (gated,) = _cmlp_mix(u, v, cmlp_w_s, (i,), bs_p[i], vg, vb_, gated, 0, tp // TILE, SEQ_CHUNK, False)
            gated, v_n = _cmlp_mix(u, v, ws_s, (i,), bs_s[i], vg, vb_, gated, tp, ts // TILE, q_s, True)
            v_out.append(v_n.reshape(bs, ls, cdim))
            mix = _matmul([(gated, 0)], cmlp_w_out, (i,), (0,), cdim, 0, d, tn, tm, F32, name="cmlp_out")

        x, xp, rt = _ln_router(x, mix, ln_mix_g[layer][None, :], ln_mix_b[layer][None, :], w_router[layer], alpha,
                               n_groups, per_group, tm_ln)
        dest, row_tok, plan = _moe_plan(rt, n_experts, n_blocks)
        xs = _dispatch(row_tok, plan[3], xp)
        hdim = moe_w_gate.shape[-1]
        h = _expert_up(plan, xs, moe_w_gate, moe_w_up, layer, min(hdim, 512))
        y_rows = _expert_down(plan, h, moe_w_down, layer, min(d, 2048))
        x, xb = _combine_ln(dest, x, rt, ln_ffn_g[layer][None, :], ln_ffn_b[layer][None, :], y_rows, alpha, tm_ln)

    y_prompt = x[:tp].reshape(bp, lp, d)
    y_sample = x[tp:].reshape(bs, ls, d)
    pool_p = jnp.stack([a for a, _ in pool_out])
    pool_s = jnp.stack([b for _, b in pool_out])
    conv_p = jnp.stack([a for a, _ in conv_out])
    conv_s = jnp.stack([b for _, b in conv_out])
    v_s = jnp.stack(v_out)
    return (y_prompt, y_sample, pool_p, pool_s, conv_p, conv_s, ssm_p, ssm_s, v_s)
```

```python
import functools
import math

import jax
import jax.numpy as jnp
from jax import lax
from jax.experimental import pallas as pl
from jax.experimental.pallas import tpu as pltpu

F32 = jnp.float32
BF16 = jnp.bfloat16
U32 = jnp.uint32
I32 = jnp.int32

POOL_WINDOWS = (2, 4, 8, 16)
POOL_BUF = max(POOL_WINDOWS) - 1
SSM_HEAD_DIM = 64
SSM_GROUPS = 8
SEQ_CHUNK = 128
MOE_TOPK = 2
PAST_LEN = 16384
LN_EPS = 1e-5
RMS_EPS = 1e-5

LANE = 128
SUBLANE = 8
V7X_VMEM_BYTES = 64 * 1024 * 1024
VMEM_LIMIT = V7X_VMEM_BYTES - 8 * 1024 * 1024
TILE = 128
MOE_BLOCK = 256

_NT = (((1,), (1,)), ((), ()))


def _cparams(sem):
    return pltpu.CompilerParams(dimension_semantics=sem, vmem_limit_bytes=VMEM_LIMIT)


def _sigmoid(x):
    return 0.5 * (jnp.tanh(0.5 * x) + 1.0)


def _silu(x):
    return x * _sigmoid(x)


def _gelu_tanh(x):
    return 0.5 * x * (1.0 + jnp.tanh(math.sqrt(2.0 / math.pi) * (x + 0.044715 * (x * x * x))))


def _softplus(x):
    return jnp.maximum(x, 0.0) + jnp.log1p(jnp.exp(-jnp.abs(x)))


def _layer_norm(y, g, b):
    mu = jnp.mean(y, axis=-1, keepdims=True)
    d = y - mu
    var = jnp.mean(d * d, axis=-1, keepdims=True)
    return d * lax.rsqrt(var + LN_EPS) * g + b


def _mm_body(*refs, n_chunks, has_bias, has_scale, act, w_t=False):
    xs = refs[:n_chunks]
    ws = refs[n_chunks:2 * n_chunks]
    pos = 2 * n_chunks
    bias_ref = scale_ref = None
    if has_bias:
        bias_ref = refs[pos]
        pos += 1
    if has_scale:
        scale_ref = refs[pos]
        pos += 1
    o_ref, wb = refs[pos], refs[pos + 1]

    @pl.when(pl.program_id(1) == 0)
    def _():
        for c in range(n_chunks):
            wb[c] = ws[c][...].astype(BF16)

    acc = None
    for c in range(n_chunks):
        if w_t:
            d = lax.dot_general(xs[c][...], wb[c], _NT, preferred_element_type=F32)
        else:
            d = jnp.dot(xs[c][...], wb[c], preferred_element_type=F32)
        acc = d if acc is None else acc + d
    if has_scale:
        acc = acc * scale_ref[...]
    if has_bias:
        acc = acc + bias_ref[...]
    if act == "gelu":
        acc = _gelu_tanh(acc)
    o_ref[...] = acc.astype(o_ref.dtype)


def _matmul(lhs, w, w_lead, w_row_blocks, kc, col0, n_cols, tn, tm, out_dtype, bias=None, act=None, name="mm",
            w_t=False):
    n_chunks = len(lhs)
    m = lhs[0][0].shape[0]
    assert m % tm == 0 and n_cols % tn == 0 and col0 % tn == 0
    cb0 = col0 // tn
    nl = len(w_lead)
    in_specs = []
    for _, cb in lhs:
        in_specs.append(pl.BlockSpec((tm, kc), functools.partial(lambda j, i, cb: (i, cb), cb=cb)))
    for rb in w_row_blocks:
        if w_t:
            in_specs.append(pl.BlockSpec((None,) * nl + (tn, kc),
                                         functools.partial(lambda j, i, rb: tuple(w_lead) + (cb0 + j, rb), rb=rb)))
        else:
            in_specs.append(pl.BlockSpec((None,) * nl + (kc, tn),
                                         functools.partial(lambda j, i, rb: tuple(w_lead) + (rb, cb0 + j), rb=rb)))
    args = [a for a, _ in lhs] + [w] * n_chunks
    if bias is not None:
        in_specs.append(pl.BlockSpec((1, tn), lambda j, i: (0, j)))
        args.append(bias)
    return pl.pallas_call(
        functools.partial(_mm_body, n_chunks=n_chunks, has_bias=bias is not None, has_scale=False, act=act, w_t=w_t),
        out_shape=jax.ShapeDtypeStruct((m, n_cols), out_dtype),
        grid=(n_cols // tn, m // tm),
        in_specs=in_specs,
        out_specs=pl.BlockSpec((tm, tn), lambda j, i: (i, j)),
        scratch_shapes=[pltpu.VMEM((n_chunks, tn, kc) if w_t else (n_chunks, kc, tn), BF16)],
        compiler_params=_cparams(("arbitrary", "arbitrary")),
        name=name,
    )(*args)


def _group_matmul(x, w, layer, scale, tm):
    m = x.shape[0]
    _, n_g, gd, _ = w.shape
    return pl.pallas_call(
        functools.partial(_mm_body, n_chunks=1, has_bias=False, has_scale=True, act=None),
        out_shape=jax.ShapeDtypeStruct((m, n_g * gd), BF16),
        grid=(n_g, m // tm),
        in_specs=[pl.BlockSpec((tm, gd), lambda g, i: (i, g)),
                  pl.BlockSpec((None, None, gd, gd), lambda g, i: (layer, g, 0, 0)),
                  pl.BlockSpec((1, gd), lambda g, i: (0, g))],
        out_specs=pl.BlockSpec((tm, gd), lambda g, i: (i, g)),
        scratch_shapes=[pltpu.VMEM((1, gd, gd), BF16)],
        compiler_params=_cparams(("arbitrary", "arbitrary")),
        name="pool_proj",
    )(x, w, scale)


def _pack_bf16_pair(y):
    half = y.shape[1] // 2
    hi = lax.bitcast_convert_type(y[:, :half].astype(BF16).astype(F32), U32)
    lo = lax.bitcast_convert_type(y[:, half:].astype(BF16).astype(F32), U32)
    return (hi & jnp.uint32(0xFFFF0000)) | (lo >> 16)


def _unpack_bf16_pair(p):
    hi = lax.bitcast_convert_type(p & jnp.uint32(0xFFFF0000), F32).astype(BF16)
    lo = lax.bitcast_convert_type(p << 16, F32).astype(BF16)
    return hi, lo


def _ln_router_body(x_ref, mix_ref, g_ref, b_ref, wr_ref, xo_ref, xp_ref, rt_ref, *, alpha, n_groups, per_group):
    y = _layer_norm(alpha * x_ref[...] + mix_ref[...], g_ref[...], b_ref[...])
    xo_ref[...] = y
    xp_ref[...] = _pack_bf16_pair(y)
    logits = jnp.dot(y.astype(BF16), wr_ref[...], preferred_element_type=F32)
    rows = logits.shape[0]
    lane = lax.broadcasted_iota(I32, (rows, LANE), 1).astype(F32)
    neg = -jnp.inf
    far = float(LANE)
    is_g = lane < n_groups
    gl = jnp.where(is_g, logits, neg)
    gmax = jnp.max(gl, axis=1, keepdims=True)
    gsel = jnp.min(jnp.where(gl == gmax, lane, far), axis=1, keepdims=True)
    gden = jnp.sum(jnp.where(is_g, jnp.exp(jnp.where(is_g, logits, gmax) - gmax), 0.0), axis=1, keepdims=True)
    gw = 1.0 / gden
    lo = n_groups + gsel * per_group
    el = jnp.where((lane >= lo) & (lane < lo + per_group), logits, neg)
    e1 = jnp.max(el, axis=1, keepdims=True)
    i1 = jnp.min(jnp.where(el == e1, lane, far), axis=1, keepdims=True)
    el2 = jnp.where(lane == i1, neg, el)
    e2 = jnp.max(el2, axis=1, keepdims=True)
    i2 = jnp.min(jnp.where(el2 == e2, lane, far), axis=1, keepdims=True)
    t = jnp.exp(e2 - e1)
    w1 = gw / (1.0 + t)
    w2 = gw * t / (1.0 + t)
    out = jnp.where(lane == 0, i1 - n_groups,
                    jnp.where(lane == 1, i2 - n_groups,
                              jnp.where(lane == 2, w1, jnp.where(lane == 3, w2, 0.0))))
    rt_ref[...] = out


def _ln_router(x, mix, g, b, w_router, alpha, n_groups, per_group, tm):
    t, d = x.shape
    row = lambda i: (i, 0)
    fixed = lambda i: (0, 0)
    return pl.pallas_call(
        functools.partial(_ln_router_body, alpha=alpha, n_groups=n_groups, per_group=per_group),
        out_shape=(jax.ShapeDtypeStruct((t, d), F32), jax.ShapeDtypeStruct((t, d // 2), U32),
                   jax.ShapeDtypeStruct((t, LANE), F32)),
        grid=(t // tm,),
        in_specs=[pl.BlockSpec((tm, d), row), pl.BlockSpec((tm, d), row), pl.BlockSpec((1, d), fixed),
                  pl.BlockSpec((1, d), fixed), pl.BlockSpec((d, LANE), fixed)],
        out_specs=(pl.BlockSpec((tm, d), row), pl.BlockSpec((tm, d // 2), row), pl.BlockSpec((tm, LANE), row)),
        compiler_params=_cparams(("arbitrary",)),
        name="ln_router",
    )(x, mix, g, b, w_router)


def _row_gather(n, src_hbm, src_row, dst, sem, wait):
    def body(g, carry):
        for u in range(SUBLANE):
            cp = pltpu.make_async_copy(src_hbm.at[pl.ds(src_row(g * SUBLANE + u), 1)], dst.at[g, pl.ds(u, 1)], sem)
            if wait:
                cp.wait()
            else:
                cp.start()
        return carry

    lax.fori_loop(0, n // SUBLANE, body, 0)


def _token_blocks(tok_ref, meta_ref, xp_hbm, xbuf, xsems):
    p, b = pl.program_id(0), pl.program_id(1)
    n_phase = pl.num_programs(0)
    n_used = meta_ref[0]
    used = b < n_used
    slot = (p * n_used + b) & 1

    def rows(blk, slot_, wait):
        _row_gather(MOE_BLOCK, xp_hbm, lambda r: tok_ref[blk * MOE_BLOCK + r], xbuf.at[slot_], xsems.at[slot_], wait)

    @pl.when(used & (p == 0) & (b == 0))
    def _():
        rows(b, slot, False)

    wrap = b + 1 >= n_used
    nxt_p = jnp.where(wrap, p + 1, p)
    nxt_b = jnp.where(wrap, 0, b + 1)

    @pl.when(used & (nxt_p < n_phase))
    def _():
        rows(nxt_b, 1 - slot, False)

    @pl.when(used)
    def _():
        rows(b, slot, True)

    return slot


def _weight_jobs(first_ref, rank_ref, dist_ref, meta_ref, w_hbms, layer, chunk, wbuf, sems):
    p, b = pl.program_id(0), pl.program_id(1)
    n_phase = pl.num_programs(0)
    nd = meta_ref[1]
    r = rank_ref[b]
    is_first = first_ref[b] == 1
    slot = (p * nd + r) & 1

    def copies(phase, rnk, slot_):
        e = dist_ref[rnk]
        col = pl.multiple_of(phase * chunk, chunk)
        return [pltpu.make_async_copy(w.at[layer, e, :, pl.ds(col, chunk)], wbuf.at[slot_, i], sems.at[slot_])
                for i, w in enumerate(w_hbms)]

    @pl.when(is_first & (p == 0) & (r == 0))
    def _():
        for cp in copies(p, r, slot):
            cp.start(priority=1)

    wrap = r + 1 >= nd
    nxt_p = jnp.where(wrap, p + 1, p)
    nxt_r = jnp.where(wrap, 0, r + 1)

    @pl.when(is_first & (nxt_p < n_phase))
    def _():
        for cp in copies(nxt_p, nxt_r, 1 - slot):
            cp.start(priority=1)

    @pl.when(is_first)
    def _():
        for cp in copies(p, r, slot):
            cp.wait()

    return is_first, slot


def _expert_up_body(first_ref, rank_ref, dist_ref, meta_ref, tok_ref, xp_hbm, wg_hbm, wu_hbm, h_ref, wbuf, wgb, wub,
                    sems, xbuf, xsems, *, layer, hc):
    xslot = _token_blocks(tok_ref, meta_ref, xp_hbm, xbuf, xsems)
    is_first, slot = _weight_jobs(first_ref, rank_ref, dist_ref, meta_ref, (wg_hbm, wu_hbm), layer, hc, wbuf, sems)
    blk = pl.program_id(1)
    n_used = meta_ref[0]

    @pl.when(is_first)
    def _():
        wgb[...] = wbuf[slot, 0].astype(BF16)
        wub[...] = wbuf[slot, 1].astype(BF16)

    @pl.when(blk < n_used)
    def _():
        hi, lo = _unpack_bf16_pair(xbuf[xslot].reshape(MOE_BLOCK, xbuf.shape[-1]))
        half = hi.shape[1]
        gate = (jnp.dot(hi, wgb[:half], preferred_element_type=F32)
                + jnp.dot(lo, wgb[half:], preferred_element_type=F32))
        up = (jnp.dot(hi, wub[:half], preferred_element_type=F32)
              + jnp.dot(lo, wub[half:], preferred_element_type=F32))
        h_ref[...] = (_silu(gate) * up).astype(BF16)

    @pl.when(blk >= n_used)
    def _():
        h_ref[...] = jnp.zeros_like(h_ref)


def _expert_up(plan, row_tok, xp, w_gate, w_up, layer, hc):
    dh = xp.shape[1]
    n_rows = row_tok.shape[0]
    d, hdim = w_gate.shape[2], w_gate.shape[3]
    n_blk = n_rows // MOE_BLOCK
    any_spec = pl.BlockSpec(memory_space=pl.ANY)
    return pl.pallas_call(
        functools.partial(_expert_up_body, layer=layer, hc=hc),
        out_shape=jax.ShapeDtypeStruct((n_rows, hdim), BF16),
        grid_spec=pltpu.PrefetchScalarGridSpec(
            num_scalar_prefetch=5,
            grid=(hdim // hc, n_blk),
            in_specs=[any_spec, any_spec, any_spec],
            out_specs=pl.BlockSpec((MOE_BLOCK, hc), lambda p, b, *_: (b, p)),
            scratch_shapes=[pltpu.VMEM((2, 2, d, hc), F32), pltpu.VMEM((d, hc), BF16), pltpu.VMEM((d, hc), BF16),
                            pltpu.SemaphoreType.DMA((2,)),
                            pltpu.VMEM((2, MOE_BLOCK // SUBLANE, SUBLANE, dh), U32), pltpu.SemaphoreType.DMA((2,))],
        ),
        compiler_params=_cparams(("arbitrary", "arbitrary")),
        name="expert_up",
    )(*plan, row_tok, xp, w_gate, w_up)


def _expert_down_body(first_ref, rank_ref, dist_ref, meta_ref, h_ref, wd_hbm, y_ref, wbuf, wdb, sems, *, layer, oc):
    is_first, slot = _weight_jobs(first_ref, rank_ref, dist_ref, meta_ref, (wd_hbm,), layer, oc, wbuf, sems)
    blk = pl.program_id(1)
    n_used = meta_ref[0]

    @pl.when(is_first)
    def _():
        wdb[...] = wbuf[slot, 0].astype(BF16)

    @pl.when(blk < n_used)
    def _():
        y_ref[...] = jnp.dot(h_ref[...], wdb[...], preferred_element_type=F32)

    @pl.when(blk >= n_used)
    def _():
        y_ref[...] = jnp.zeros_like(y_ref)


def _expert_down(plan, h, w_down, layer, oc):
    n_rows, hdim = h.shape
    d = w_down.shape[3]
    n_blk = n_rows // MOE_BLOCK
    return pl.pallas_call(
        functools.partial(_expert_down_body, layer=layer, oc=oc),
        out_shape=jax.ShapeDtypeStruct((n_rows, d), F32),
        grid_spec=pltpu.PrefetchScalarGridSpec(
            num_scalar_prefetch=4,
            grid=(d // oc, n_blk),
            in_specs=[pl.BlockSpec((MOE_BLOCK, hdim), lambda q, b, *_: (b, 0)), pl.BlockSpec(memory_space=pl.ANY)],
            out_specs=pl.BlockSpec((MOE_BLOCK, oc), lambda q, b, *_: (b, q)),
            scratch_shapes=[pltpu.VMEM((2, 1, hdim, oc), F32), pltpu.VMEM((hdim, oc), BF16),
                            pltpu.SemaphoreType.DMA((2,))],
        ),
        compiler_params=_cparams(("arbitrary", "arbitrary")),
        name="expert_down",
    )(*plan, h, w_down)


def _combine_ln_body(dest_ref, x_ref, rt_ref, g_ref, b_ref, y_hbm, xo_ref, xb_ref, rows, sems, *, alpha, tm):
    s = pl.program_id(0)

    def gather(step, wait):
        slot = step % 2
        for k in range(MOE_TOPK):
            _row_gather(tm, y_hbm, lambda r: dest_ref[(step * tm + r) * MOE_TOPK + k], rows.at[slot, k],
                        sems.at[slot], wait)

    @pl.when(s == 0)
    def _():
        gather(s, False)

    @pl.when(s + 1 < pl.num_programs(0))
    def _():
        gather(s + 1, False)

    gather(s, True)
    slot = s % 2
    rt = rt_ref[...]
    x = x_ref[...]
    ffn = rt[:, 2:3] * rows[slot, 0].reshape(x.shape) + rt[:, 3:4] * rows[slot, 1].reshape(x.shape)
    y = _layer_norm(alpha * x + ffn, g_ref[...], b_ref[...])
    xo_ref[...] = y
    xb_ref[...] = y.astype(BF16)


def _combine_ln(dest, x, rt, g, b, y_rows, alpha, tm):
    t, d = x.shape
    row = lambda i, de: (i, 0)
    fixed = lambda i, de: (0, 0)
    return pl.pallas_call(
        functools.partial(_combine_ln_body, alpha=alpha, tm=tm),
        out_shape=(jax.ShapeDtypeStruct((t, d), F32), jax.ShapeDtypeStruct((t, d), BF16)),
        grid_spec=pltpu.PrefetchScalarGridSpec(
            num_scalar_prefetch=1,
            grid=(t // tm,),
            in_specs=[pl.BlockSpec((tm, d), row), pl.BlockSpec((tm, LANE), row), pl.BlockSpec((1, d), fixed),
                      pl.BlockSpec((1, d), fixed), pl.BlockSpec(memory_space=pl.ANY)],
            out_specs=(pl.BlockSpec((tm, d), row), pl.BlockSpec((tm, d), row)),
            scratch_shapes=[pltpu.VMEM((2, MOE_TOPK, tm // SUBLANE, SUBLANE, d), F32), pltpu.SemaphoreType.DMA((2,))],
        ),
        compiler_params=_cparams(("arbitrary",)),
        name="moe_combine_ln",
    )(dest, x, rt, g, b, y_rows)


def _moe_plan(rt, n_experts, n_blocks):
    n_assign = rt.shape[0] * MOE_TOPK
    e = rt[:, :MOE_TOPK].astype(I32).reshape(-1)
    onehot = (e[:, None] == jnp.arange(n_experts, dtype=I32)[None, :]).astype(I32)
    csum = jnp.cumsum(onehot, axis=0)
    rank_in_e = jnp.sum(csum * onehot, axis=1) - 1
    counts = csum[-1]
    padded = (counts + MOE_BLOCK - 1) // MOE_BLOCK * MOE_BLOCK
    pad_end = jnp.cumsum(padded)
    pad_start = pad_end - padded
    dest = (jnp.sum(onehot * pad_start[None, :], axis=1) + rank_in_e).astype(I32)
    row_tok = jnp.zeros((n_blocks * MOE_BLOCK,), I32).at[dest].set(jnp.arange(n_assign, dtype=I32) // MOE_TOPK)
    blk = jnp.arange(n_blocks, dtype=I32)
    blk_e = jnp.minimum(jnp.searchsorted(pad_end, blk * MOE_BLOCK, side="right"), n_experts - 1).astype(I32)
    n_used = (pad_end[-1] // MOE_BLOCK).astype(I32)
    prev_e = jnp.concatenate([jnp.full((1,), -1, I32), blk_e[:-1]])
    first = ((blk < n_used) & (blk_e != prev_e)).astype(I32)
    rank = jnp.maximum(jnp.cumsum(first) - 1, 0).astype(I32)
    dist_e = jnp.argsort(counts == 0, stable=True).astype(I32)
    meta = jnp.stack([n_used, jnp.sum(first).astype(I32)])
    return dest, row_tok, (first, rank, dist_e, meta)


def _pool_body(u_ref, st_ref, alias_ref, o_ref, ext, *, q, nb, start_pos, gd):
    del alias_ref
    c = pl.program_id(1)
    head = POOL_BUF + 1

    @pl.when(c == 0)
    def _():
        for i in range(nb):
            ext[i, 1:head] = st_ref[i]

    pos = (start_pos + c * q + lax.broadcasted_iota(I32, (q, 1), 0) + 1).astype(F32)
    for i in range(nb):
        ext[i, head:head + q] = u_ref[i * q:(i + 1) * q]
    for i in range(nb):
        for gi, w in enumerate(POOL_WINDOWS):
            sl = slice(gi * gd, (gi + 1) * gd)
            acc = ext[i, head:head + q, sl]
            cur = acc
            for k in range(1, w):
                acc = acc + ext[i, head - k:head - k + q, sl]
            pooled = acc / jnp.minimum(pos, float(w))
            o_ref[i * q:(i + 1) * q, sl] = (pooled - cur).astype(o_ref.dtype)
    for i in range(nb):
        ext[i, 0:head] = ext[i, q:q + head]


def _pool_mix(u, state, lead, out_buf, row0, n_tiles_outer, nc, q, nb, start_pos):
    t, cdim = u.shape
    gd = cdim // len(POOL_WINDOWS)
    rb0 = row0 // TILE
    nl = len(lead)
    rows = lambda b, c: (rb0 + b * nc + c, 0)
    return pl.pallas_call(
        functools.partial(_pool_body, q=q, nb=nb, start_pos=start_pos, gd=gd),
        out_shape=jax.ShapeDtypeStruct(out_buf.shape, out_buf.dtype),
        grid=(n_tiles_outer, nc),
        in_specs=[pl.BlockSpec((TILE, cdim), rows),
                  pl.BlockSpec((None,) * nl + (nb, POOL_BUF, cdim), lambda b, c: tuple(lead) + (b, 0, 0)),
                  pl.BlockSpec(memory_space=pl.ANY)],
        out_specs=pl.BlockSpec((TILE, cdim), rows),
        scratch_shapes=[pltpu.VMEM((nb, POOL_BUF + 1 + q, cdim), F32)],
        input_output_aliases={2: 0},
        compiler_params=_cparams(("arbitrary", "arbitrary")),
        name="pool_mix",
    )(u, state, out_buf)


def _ssd_body(xs_ref, bm_ref, cm_ref, z_ref, dt_ref, sx_ref, sb_ref, sc_ref, h0_ref,
              wx_ref, wb_ref, wc_ref, bx_ref, bb_ref, bc_ref, dtb_ref, alog_ref, dexp_ref, nw_ref, alias_ref,
              halias_ref, yb_ref, hout_ref, hs, ex, eb, ec, *, q, nb, hpg, kconv):
    del alias_ref, halias_ref
    c = pl.program_id(2)
    p_dim = SSM_HEAD_DIM
    off = SUBLANE - (kconv - 1)

    @pl.when(c == 0)
    def _():
        hs[...] = h0_ref[...].reshape(hs.shape)
        for i in range(nb):
            ex[i, off:SUBLANE] = sx_ref[i]
            eb[i, off:SUBLANE] = sb_ref[i]
            ec[i, off:SUBLANE] = sc_ref[i]

    def conv(raw_ref, e_ref, w_ref, bias_ref):
        outs = []
        for i in range(nb):
            e_ref[i, SUBLANE:SUBLANE + q] = raw_ref[i * q:(i + 1) * q]
            acc = bias_ref[...] + e_ref[i, off:off + q] * w_ref[0:1, :]
            for k in range(1, kconv):
                acc = acc + e_ref[i, off + k:off + k + q] * w_ref[k:k + 1, :]
            outs.append(acc)
            e_ref[i, 0:SUBLANE] = e_ref[i, q:q + SUBLANE]
        o = outs[0] if nb == 1 else jnp.concatenate(outs, axis=0)
        return _silu(o)

    xs = conv(xs_ref, ex, wx_ref, bx_ref)
    bm = conv(bm_ref, eb, wb_ref, bb_ref)
    cm = conv(cm_ref, ec, wc_ref, bc_ref)
    dt = _softplus(dt_ref[...] + dtb_ref[...])
    dta = dt * (-jnp.exp(alog_ref[...]))
    row = lax.broadcasted_iota(I32, (TILE, TILE), 0)
    col = lax.broadcasted_iota(I32, (TILE, TILE), 1)
    same = (row // q) == (col // q)
    causal = same & (col <= row)
    hp = lax.Precision.HIGHEST
    acs = jnp.dot(causal.astype(F32), dta, preferred_element_type=F32, precision=hp)
    tot = jnp.dot(same.astype(F32), dta, preferred_element_type=F32, precision=hp)
    acs_t = acs.T
    tot_t = tot.T
    e_acs = jnp.exp(acs)
    to_end = jnp.exp(tot - acs)
    bm_b = bm.astype(BF16)
    cm_b = cm.astype(BF16)
    scores = lax.dot_general(cm_b, bm_b, _NT, preferred_element_type=F32)
    lo_half = lax.broadcasted_iota(I32, (TILE, LANE), 1) < p_dim

    y_parts, xw_t_parts, ea_parts = [], [], []
    for p in range(hpg // 2):
        j0, j1 = 2 * p, 2 * p + 1

        def expand(v):
            return jnp.where(lo_half, v[:, j0:j0 + 1], v[:, j1:j1 + 1])

        xs_p = xs[:, p * LANE:(p + 1) * LANE]
        xdt = xs_p * expand(dt)
        xdt_b = xdt.astype(BF16)
        ys = []
        for j in (j0, j1):
            decay = jnp.exp(jnp.where(causal, acs[:, j:j + 1] - acs_t[j:j + 1, :], -jnp.inf))
            ys.append(jnp.dot((scores * decay).astype(BF16), xdt_b, preferred_element_type=F32))
        y_parts.append(jnp.where(lo_half, ys[0], ys[1]))
        xw_t_parts.append((xdt * expand(to_end)).T.astype(BF16))
        ea_parts.append(expand(e_acs))
    y_diag = jnp.concatenate(y_parts, axis=1)
    xw_t = jnp.concatenate(xw_t_parts, axis=0)
    ea = jnp.concatenate(ea_parts, axis=1)

    sub = lax.broadcasted_iota(I32, (TILE, 1), 0) // q
    y_off = None
    for i in range(nb):
        h_i = hs[i]
        full = lax.dot_general(cm_b, h_i.astype(BF16), _NT, preferred_element_type=F32)
        y_off = full if nb == 1 else (jnp.where(sub == i, full, 0.0) + (0.0 if y_off is None else y_off))
        bm_i = bm_b if nb == 1 else jnp.where(sub == i, bm, 0.0).astype(BF16)
        st = jnp.dot(xw_t, bm_i, preferred_element_type=F32)
        for j in range(hpg):
            rs = slice(j * p_dim, (j + 1) * p_dim)
            cd = jnp.exp(tot_t[j:j + 1, i * q:i * q + 1])
            hs[i, rs, :] = h_i[rs, :] * cd + st[rs, :]

    y = y_diag + y_off * ea + xs * dexp_ref[...]
    y = y * _silu(z_ref[...])
    ms = jnp.mean(y * y, axis=1, keepdims=True)
    yb_ref[...] = (y * lax.rsqrt(ms + RMS_EPS) * nw_ref[...]).astype(yb_ref.dtype)

    @pl.when(c == pl.num_programs(2) - 1)
    def _():
        hout_ref[...] = hs[...].reshape(hout_ref.shape)


def _ssd_mix(xbc, z, dtr, conv_state, conv_lead, h0, h0_lead, conv_w, conv_b, dtb, alog, dexp, nw, layer,
             out_buf, h_buf, row0, n_outer, nc, q, nb, inner):
    n_state = LANE
    p_dim = SSM_HEAD_DIM
    gw = inner // SSM_GROUPS
    hpg = gw // SSM_HEAD_DIM
    assert hpg % 2 == 0 and gw % LANE == 0
    kconv = conv_w.shape[1]
    rb0 = row0 // TILE
    b_blk = inner // n_state
    c_blk = (inner + SSM_GROUPS * n_state) // n_state
    ncl, nhl = len(conv_lead), len(h0_lead)

    def rows(colf):
        return lambda b, g, c: (rb0 + b * nc + c, colf(g))

    def cstate(width, colf):
        return pl.BlockSpec((None,) * ncl + (nb, kconv - 1, width), lambda b, g, c: tuple(conv_lead) + (b, 0, colf(g)))

    def wspec(nrows, width, colf):
        return pl.BlockSpec((None, nrows, width), lambda b, g, c: (layer, 0, colf(g)))

    xcol = lambda g: g
    bcol = lambda g: b_blk + g
    ccol = lambda g: c_blk + g
    in_specs = [
        pl.BlockSpec((TILE, gw), rows(xcol)), pl.BlockSpec((TILE, n_state), rows(bcol)),
        pl.BlockSpec((TILE, n_state), rows(ccol)), pl.BlockSpec((TILE, gw), rows(xcol)),
        pl.BlockSpec((TILE, LANE), rows(xcol)),
        cstate(gw, xcol), cstate(n_state, bcol), cstate(n_state, ccol),
        pl.BlockSpec((None,) * nhl + (nb, hpg, p_dim, n_state), lambda b, g, c: tuple(h0_lead) + (b, g, 0, 0)),
        wspec(kconv, gw, xcol), wspec(kconv, n_state, bcol), wspec(kconv, n_state, ccol),
        wspec(1, gw, xcol), wspec(1, n_state, bcol), wspec(1, n_state, ccol),
        wspec(1, LANE, xcol), wspec(1, LANE, xcol), wspec(1, gw, xcol), wspec(1, gw, xcol),
        pl.BlockSpec(memory_space=pl.ANY), pl.BlockSpec(memory_space=pl.ANY),
    ]
    return pl.pallas_call(
        functools.partial(_ssd_body, q=q, nb=nb, hpg=hpg, kconv=kconv),
        out_shape=(jax.ShapeDtypeStruct(out_buf.shape, out_buf.dtype),
                   jax.ShapeDtypeStruct(h_buf.shape, h_buf.dtype)),
        grid=(n_outer, SSM_GROUPS, nc),
        in_specs=in_specs,
        out_specs=(pl.BlockSpec((TILE, gw), rows(xcol)),
                   pl.BlockSpec((None, nb, hpg, p_dim, n_state), lambda b, g, c: (layer, b, g, 0, 0))),
        scratch_shapes=[pltpu.VMEM((nb, gw, n_state), F32), pltpu.VMEM((nb, SUBLANE + q, gw), F32),
                        pltpu.VMEM((nb, SUBLANE + q, n_state), F32), pltpu.VMEM((nb, SUBLANE + q, n_state), F32)],
        input_output_aliases={19: 0, 20: 1},
        compiler_params=_cparams(("arbitrary", "arbitrary", "arbitrary")),
        name="ssd_mix",
    )(xbc, xbc, xbc, z, dtr, conv_state, conv_state, conv_state, h0, conv_w, conv_w, conv_w, conv_b, conv_b, conv_b,
      dtb, alog, dexp, nw, out_buf, h_buf)


def _cmlp_body(u_ref, v_ref, ws_ref, bs_ref, g_ref, b_ref, alias_ref, *out_refs, q, n_heads, emit_v):
    del alias_ref
    o_ref = out_refs[0]
    vn = _layer_norm(v_ref[...], g_ref[...], b_ref[...])
    if emit_v:
        out_refs[1][...] = vn
    vb = vn.astype(BF16)
    row = lax.broadcasted_iota(I32, (TILE, TILE), 0)
    col = lax.broadcasted_iota(I32, (TILE, TILE), 1)
    causal = ((row // q) == (col // q)) & (col <= row)
    hd = vb.shape[1] // n_heads
    bs = bs_ref[...]
    for h in range(n_heads):
        sl = slice(h * hd, (h + 1) * hd)
        wm = jnp.where(causal, ws_ref[h], 0.0).astype(BF16)
        mixed = jnp.dot(wm, vb[:, sl], preferred_element_type=F32) + bs[:, h:h + 1]
        o_ref[:, sl] = (u_ref[:, sl].astype(F32) * mixed).astype(o_ref.dtype)


def _cmlp_mix(u, v, ws, ws_lead, bs_t, vg, vb, out_buf, row0, n_tiles, q, emit_v):
    t, cdim = u.shape
    n_heads = ws.shape[-3]
    rb0 = row0 // TILE
    nl = len(ws_lead)
    rows = lambda i: (rb0 + i, 0)
    fixed = lambda i: (0, 0)
    out_shape = [jax.ShapeDtypeStruct(out_buf.shape, out_buf.dtype)]
    out_specs = [pl.BlockSpec((TILE, cdim), rows)]
    if emit_v:
        out_shape.append(jax.ShapeDtypeStruct((n_tiles * TILE, cdim), F32))
        out_specs.append(pl.BlockSpec((TILE, cdim), lambda i: (i, 0)))
    return pl.pallas_call(
        functools.partial(_cmlp_body, q=q, n_heads=n_heads, emit_v=emit_v),
        out_shape=tuple(out_shape),
        grid=(n_tiles,),
        in_specs=[pl.BlockSpec((TILE, cdim), rows), pl.BlockSpec((TILE, cdim), rows),
                  pl.BlockSpec((None,) * nl + (n_heads, TILE, TILE), lambda i: tuple(ws_lead) + (0, 0, 0)),
                  pl.BlockSpec((TILE, LANE), fixed), pl.BlockSpec((1, cdim), fixed), pl.BlockSpec((1, cdim), fixed),
                  pl.BlockSpec(memory_space=pl.ANY)],
        out_specs=tuple(out_specs),
        input_output_aliases={6: 0},
        compiler_params=_cparams(("arbitrary",)),
        name="cmlp_mix",
    )(u, v, ws, bs_t, vg, vb, out_buf)


def _pick(n, target):
    t = min(n, target)
    while n % t:
        t -= SUBLANE
    return t


def kernel(x_prompt, x_sample, state_pool, state_conv, state_ssm, w_in_even, pool_w, pool_scale, conv_w, conv_b, dt_bias, a_log, d_skip, ssm_norm_w, w_out_even, cmlp_w_in, cmlp_b_in, cmlp_v_g, cmlp_v_b, cmlp_w_s, cmlp_b_s, cmlp_w_out, ln_mix_g, ln_mix_b, ln_ffn_g, ln_ffn_b, moe_w_group, moe_w_expert, moe_w_gate, moe_w_up, moe_w_down):
    bp, lp, d = x_prompt.shape
    bs, ls, _ = x_sample.shape
    depth = ln_mix_g.shape[0]
    alpha = float((2 * depth) ** 0.25)
    tp, ts = bp * lp, bs * ls
    t = tp + ts
    pool_dim = state_pool.shape[-1]
    conv_dim = state_conv.shape[-1]
    n_heads_ssm = state_ssm.shape[2]
    inner = n_heads_ssm * SSM_HEAD_DIM
    n_state = state_ssm.shape[-1]
    hpg = n_heads_ssm // SSM_GROUPS
    n_groups = moe_w_group.shape[-1]
    n_experts = moe_w_expert.shape[-1]
    per_group = n_experts // n_groups
    cdim = cmlp_w_out.shape[1]
    c_heads = cmlp_w_s.shape[1]
    assert n_state == LANE and SEQ_CHUNK == TILE and lp % TILE == 0 and TILE % ls == 0 and ts % TILE == 0
    assert n_groups + n_experts <= LANE and hpg <= LANE
    q_s = math.gcd(ls, SEQ_CHUNK)
    assert q_s == ls
    nb_s = TILE // q_s
    n_blocks = -(-(t * MOE_TOPK) // MOE_BLOCK) + n_experts

    tm = _pick(t, 1024)
    tm_ln = _pick(t, 256)
    tn = 512

    x = jnp.concatenate([x_prompt.reshape(tp, d), x_sample.reshape(ts, d)], axis=0)
    xb = x.astype(BF16)

    dt_col = pool_dim + inner + conv_dim

    def group_pad(v):
        lead = v.shape[:-1]
        v = v.reshape(lead + (SSM_GROUPS, hpg))
        v = jnp.pad(v, [(0, 0)] * len(lead) + [(0, 0), (0, LANE - hpg)])
        return v.reshape(lead + (SSM_GROUPS * LANE,))

    w_in_t = jnp.swapaxes(w_in_even, 1, 2)
    w_dt_t = w_in_t[:, dt_col:].reshape(-1, SSM_GROUPS, hpg, d)
    w_dt_t = jnp.pad(w_dt_t, ((0, 0), (0, 0), (0, LANE - hpg), (0, 0))).reshape(-1, SSM_GROUPS * LANE, d)
    dtb_p = group_pad(dt_bias)[:, None, :]
    alog_p = group_pad(a_log)[:, None, :]
    dexp = jnp.repeat(d_skip, SSM_HEAD_DIM, axis=-1)[:, None, :]
    nw3 = ssm_norm_w[:, None, :]
    conv_b3 = conv_b[:, None, :]
    w_router = jnp.concatenate(
        [moe_w_group, moe_w_expert, jnp.zeros((depth, d, LANE - n_groups - n_experts), F32)], axis=-1).astype(BF16)
    zero_pool = jnp.zeros((bp, POOL_BUF, pool_dim), F32)
    zero_conv = jnp.zeros((bp, state_conv.shape[2], conv_dim), F32)
    n_even = state_ssm.shape[0]
    zero_ssm = jnp.zeros((bp,) + state_ssm.shape[2:], F32)
    ssm_p = jnp.zeros((n_even, bp) + state_ssm.shape[2:], F32)
    ssm_s = jnp.zeros(state_ssm.shape, F32)
    eye = jnp.eye(nb_s, dtype=F32)
    ws_s = jnp.einsum("ab,lhts->lhatbs", eye, cmlp_w_s[:, :, :q_s, :q_s]).reshape(-1, c_heads, TILE, TILE)
    bs_p = jnp.pad(jnp.swapaxes(cmlp_b_s, 1, 2), ((0, 0), (0, 0), (0, LANE - c_heads)))
    bs_s = jnp.tile(bs_p[:, :q_s], (1, nb_s, 1))

    def last_rows(a, row0, n_seq, seq_len, n):
        return jnp.stack([a[row0 + (s + 1) * seq_len - n:row0 + (s + 1) * seq_len] for s in range(n_seq)])

    def new_state(old, a, n):
        cur = a[tp:].reshape(bs, ls, a.shape[1])
        ext = cur if ls >= n else jnp.concatenate([old[:, ls:], cur], axis=1)
        return last_rows(a, 0, bp, lp, n), ext[:, ext.shape[1] - n:]

    pool_out, conv_out, v_out = [], [], []
    for layer in range(depth):
        i = layer // 2
        if layer % 2 == 0:
            lhs = [(xb, 0)]
            u_a = _matmul(lhs, w_in_t, (i,), (0,), d, 0, pool_dim, tn, tm, F32, name="in_pool", w_t=True)
            z = _matmul(lhs, w_in_t, (i,), (0,), d, pool_dim, inner, tn, tm, F32, name="in_gate", w_t=True)
            xbc = _matmul(lhs, w_in_t, (i,), (0,), d, pool_dim + inner, conv_dim, tn, tm, F32, name="in_conv",
                          w_t=True)
            dtr = _matmul(lhs, w_dt_t, (i,), (0,), d, 0, SSM_GROUPS * LANE, tn, tm, F32, name="in_dt", w_t=True)

            mixed = jnp.zeros((t, pool_dim), BF16)
            mixed = _pool_mix(u_a, zero_pool, (), mixed, 0, bp, lp // TILE, TILE, 1, 0)
            mixed = _pool_mix(u_a, state_pool, (i,), mixed, tp, ts // TILE, 1, q_s, nb_s, PAST_LEN)
            y_a = _group_matmul(mixed, pool_w, i, pool_scale[i][None, :], tm)

            y_b = jnp.zeros((t, inner), BF16)
            y_b, ssm_p = _ssd_mix(xbc, z, dtr, zero_conv, (), zero_ssm, (), conv_w, conv_b3, dtb_p, alog_p, dexp, nw3,
                                  i, y_b, ssm_p, 0, bp, lp // TILE, TILE, 1, inner)
            y_b, ssm_s = _ssd_mix(xbc, z, dtr, state_conv, (i,), state_ssm, (i,), conv_w, conv_b3, dtb_p, alog_p,
                                  dexp, nw3, i, y_b, ssm_s, tp, ts // TILE, 1, q_s, nb_s, inner)

            kc = pool_dim
            assert inner % kc == 0
            lhs_out = [(y_a, 0)] + [(y_b, k) for k in range(inner // kc)]
            mix = _matmul(lhs_out, w_out_even, (i,), tuple(range(1 + inner // kc)), kc, 0, d, tn, _pick(t, 256), F32,
                          name="out_even")

            pool_out.append(new_state(state_pool[i], u_a, POOL_BUF))
            conv_out.append(new_state(state_conv[i], xbc, state_conv.shape[2]))
        else:
            lhs = [(xb, 0)]
            bias = cmlp_b_in[i][None, :]
            u = _matmul(lhs, cmlp_w_in, (i,), (0,), d, 0, cdim, tn, tm, F32, bias=bias[:, :cdim], act="gelu",
                        name="cmlp_in_u")
            v = _matmul(lhs, cmlp_w_in, (i,), (0,), d, cdim, cdim, tn, tm, F32, bias=bias[:, cdim:], act="gelu",
                        name="cmlp_in_v")
            vg, vb_ = cmlp_v_g[i][None, :], cmlp_v_b[i][None, :]
            gated = jnp.zeros((t, cdim), BF16)
            (gated,) = _cmlp_mix(u, v, cmlp_w_s, (i,), bs_p[i], vg, vb_, gated, 0, tp // TILE, SEQ_CHUNK, False)
            gated, v_n = _cmlp_mix(u, v, ws_s, (i,), bs_s[i], vg, vb_, gated, tp, ts // TILE, q_s, True)
            v_out.append(v_n.reshape(bs, ls, cdim))
            mix = _matmul([(gated, 0)], cmlp_w_out, (i,), (0,), cdim, 0, d, tn, tm, F32, name="cmlp_out")

        x, xp, rt = _ln_router(x, mix, ln_mix_g[layer][None, :], ln_mix_b[layer][None, :], w_router[layer], alpha,
                               n_groups, per_group, tm_ln)
        dest, row_tok, plan = _moe_plan(rt, n_experts, n_blocks)
        hdim = moe_w_gate.shape[-1]
        h = _expert_up(plan, row_tok, xp, moe_w_gate, moe_w_up, layer, min(hdim, 512))
        y_rows = _expert_down(plan, h, moe_w_down, layer, min(d, 2048))
        x, xb = _combine_ln(dest, x, rt, ln_ffn_g[layer][None, :], ln_ffn_b[layer][None, :], y_rows, alpha, tm_ln)

    y_prompt = x[:tp].reshape(bp, lp, d)
    y_sample = x[tp:].reshape(bs, ls, d)
    pool_p = jnp.stack([a for a, _ in pool_out])
    pool_s = jnp.stack([b for _, b in pool_out])
    conv_p = jnp.stack([a for a, _ in conv_out])
    conv_s = jnp.stack([b for _, b in conv_out])
    v_s = jnp.stack(v_out)
    return (y_prompt, y_sample, pool_p, pool_s, conv_p, conv_s, ssm_p, ssm_s, v_s)
```

```python
import functools
import math

import jax
import jax.numpy as jnp
from jax import lax
from jax.experimental import pallas as pl
from jax.experimental.pallas import tpu as pltpu

F32 = jnp.float32
BF16 = jnp.bfloat16
U32 = jnp.uint32
I32 = jnp.int32

POOL_WINDOWS = (2, 4, 8, 16)
POOL_BUF = max(POOL_WINDOWS) - 1
SSM_HEAD_DIM = 64
SSM_GROUPS = 8
SEQ_CHUNK = 128
MOE_TOPK = 2
PAST_LEN = 16384
LN_EPS = 1e-5
RMS_EPS = 1e-5

LANE = 128
SUBLANE = 8
V7X_VMEM_BYTES = 64 * 1024 * 1024
VMEM_LIMIT = V7X_VMEM_BYTES - 8 * 1024 * 1024
TILE = 128
MOE_BLOCK = 256

_NT = (((1,), (1,)), ((), ()))


def _cparams(sem):
    return pltpu.CompilerParams(dimension_semantics=sem, vmem_limit_bytes=VMEM_LIMIT)


def _sigmoid(x):
    return 0.5 * (jnp.tanh(0.5 * x) + 1.0)


def _silu(x):
    return x * _sigmoid(x)


def _gelu_tanh(x):
    return 0.5 * x * (1.0 + jnp.tanh(math.sqrt(2.0 / math.pi) * (x + 0.044715 * (x * x * x))))


def _softplus(x):
    return jnp.maximum(x, 0.0) + jnp.log1p(jnp.exp(-jnp.abs(x)))


def _layer_norm(y, g, b):
    mu = jnp.mean(y, axis=-1, keepdims=True)
    d = y - mu
    var = jnp.mean(d * d, axis=-1, keepdims=True)
    return d * lax.rsqrt(var + LN_EPS) * g + b


def _mm_body(*refs, n_chunks, has_bias, has_scale, act, w_t=False):
    xs = refs[:n_chunks]
    ws = refs[n_chunks:2 * n_chunks]
    pos = 2 * n_chunks
    bias_ref = scale_ref = None
    if has_bias:
        bias_ref = refs[pos]
        pos += 1
    if has_scale:
        scale_ref = refs[pos]
        pos += 1
    o_ref, wb = refs[pos], refs[pos + 1]

    @pl.when(pl.program_id(1) == 0)
    def _():
        for c in range(n_chunks):
            wb[c] = ws[c][...].astype(BF16)

    acc = None
    for c in range(n_chunks):
        if w_t:
            d = lax.dot_general(xs[c][...], wb[c], _NT, preferred_element_type=F32)
        else:
            d = jnp.dot(xs[c][...], wb[c], preferred_element_type=F32)
        acc = d if acc is None else acc + d
    if has_scale:
        acc = acc * scale_ref[...]
    if has_bias:
        acc = acc + bias_ref[...]
    if act == "gelu":
        acc = _gelu_tanh(acc)
    o_ref[...] = acc.astype(o_ref.dtype)


def _matmul(lhs, w, w_lead, w_row_blocks, kc, col0, n_cols, tn, tm, out_dtype, bias=None, act=None, name="mm",
            w_t=False):
    n_chunks = len(lhs)
    m = lhs[0][0].shape[0]
    assert m % tm == 0 and n_cols % tn == 0 and col0 % tn == 0
    cb0 = col0 // tn
    nl = len(w_lead)
    in_specs = []
    for _, cb in lhs:
        in_specs.append(pl.BlockSpec((tm, kc), functools.partial(lambda j, i, cb: (i, cb), cb=cb)))
    for rb in w_row_blocks:
        if w_t:
            in_specs.append(pl.BlockSpec((None,) * nl + (tn, kc),
                                         functools.partial(lambda j, i, rb: tuple(w_lead) + (cb0 + j, rb), rb=rb)))
        else:
            in_specs.append(pl.BlockSpec((None,) * nl + (kc, tn),
                                         functools.partial(lambda j, i, rb: tuple(w_lead) + (rb, cb0 + j), rb=rb)))
    args = [a for a, _ in lhs] + [w] * n_chunks
    if bias is not None:
        in_specs.append(pl.BlockSpec((1, tn), lambda j, i: (0, j)))
        args.append(bias)
    return pl.pallas_call(
        functools.partial(_mm_body, n_chunks=n_chunks, has_bias=bias is not None, has_scale=False, act=act, w_t=w_t),
        out_shape=jax.ShapeDtypeStruct((m, n_cols), out_dtype),
        grid=(n_cols // tn, m // tm),
        in_specs=in_specs,
        out_specs=pl.BlockSpec((tm, tn), lambda j, i: (i, j)),
        scratch_shapes=[pltpu.VMEM((n_chunks, tn, kc) if w_t else (n_chunks, kc, tn), BF16)],
        compiler_params=_cparams(("arbitrary", "arbitrary")),
        name=name,
    )(*args)


def _group_matmul(x, w, layer, scale, tm):
    m = x.shape[0]
    _, n_g, gd, _ = w.shape
    return pl.pallas_call(
        functools.partial(_mm_body, n_chunks=1, has_bias=False, has_scale=True, act=None),
        out_shape=jax.ShapeDtypeStruct((m, n_g * gd), BF16),
        grid=(n_g, m // tm),
        in_specs=[pl.BlockSpec((tm, gd), lambda g, i: (i, g)),
                  pl.BlockSpec((None, None, gd, gd), lambda g, i: (layer, g, 0, 0)),
                  pl.BlockSpec((1, gd), lambda g, i: (0, g))],
        out_specs=pl.BlockSpec((tm, gd), lambda g, i: (i, g)),
        scratch_shapes=[pltpu.VMEM((1, gd, gd), BF16)],
        compiler_params=_cparams(("arbitrary", "arbitrary")),
        name="pool_proj",
    )(x, w, scale)


def _pack_bf16_pair(y):
    half = y.shape[1] // 2
    hi = lax.bitcast_convert_type(y[:, :half].astype(BF16).astype(F32), U32)
    lo = lax.bitcast_convert_type(y[:, half:].astype(BF16).astype(F32), U32)
    return (hi & jnp.uint32(0xFFFF0000)) | (lo >> 16)


def _unpack_bf16_pair(p):
    hi = lax.bitcast_convert_type(p & jnp.uint32(0xFFFF0000), F32).astype(BF16)
    lo = lax.bitcast_convert_type(p << 16, F32).astype(BF16)
    return hi, lo


def _ln_router_body(x_ref, mix_ref, g_ref, b_ref, wr_ref, xo_ref, xp_ref, rt_ref, *, alpha, n_groups, per_group):
    y = _layer_norm(alpha * x_ref[...] + mix_ref[...], g_ref[...], b_ref[...])
    xo_ref[...] = y
    xp_ref[...] = _pack_bf16_pair(y)
    logits = jnp.dot(y.astype(BF16), wr_ref[...], preferred_element_type=F32)
    rows = logits.shape[0]
    lane = lax.broadcasted_iota(I32, (rows, LANE), 1).astype(F32)
    neg = -jnp.inf
    far = float(LANE)
    is_g = lane < n_groups
    gl = jnp.where(is_g, logits, neg)
    gmax = jnp.max(gl, axis=1, keepdims=True)
    gsel = jnp.min(jnp.where(gl == gmax, lane, far), axis=1, keepdims=True)
    gden = jnp.sum(jnp.where(is_g, jnp.exp(jnp.where(is_g, logits, gmax) - gmax), 0.0), axis=1, keepdims=True)
    gw = 1.0 / gden
    lo = n_groups + gsel * per_group
    el = jnp.where((lane >= lo) & (lane < lo + per_group), logits, neg)
    e1 = jnp.max(el, axis=1, keepdims=True)
    i1 = jnp.min(jnp.where(el == e1, lane, far), axis=1, keepdims=True)
    el2 = jnp.where(lane == i1, neg, el)
    e2 = jnp.max(el2, axis=1, keepdims=True)
    i2 = jnp.min(jnp.where(el2 == e2, lane, far), axis=1, keepdims=True)
    t = jnp.exp(e2 - e1)
    w1 = gw / (1.0 + t)
    w2 = gw * t / (1.0 + t)
    out = jnp.where(lane == 0, i1 - n_groups,
                    jnp.where(lane == 1, i2 - n_groups,
                              jnp.where(lane == 2, w1, jnp.where(lane == 3, w2, 0.0))))
    rt_ref[...] = out


def _ln_router(x, mix, g, b, w_router, alpha, n_groups, per_group, tm):
    t, d = x.shape
    row = lambda i: (i, 0)
    fixed = lambda i: (0, 0)
    return pl.pallas_call(
        functools.partial(_ln_router_body, alpha=alpha, n_groups=n_groups, per_group=per_group),
        out_shape=(jax.ShapeDtypeStruct((t, d), F32), jax.ShapeDtypeStruct((t, d // 2), U32),
                   jax.ShapeDtypeStruct((t, LANE), F32)),
        grid=(t // tm,),
        in_specs=[pl.BlockSpec((tm, d), row), pl.BlockSpec((tm, d), row), pl.BlockSpec((1, d), fixed),
                  pl.BlockSpec((1, d), fixed), pl.BlockSpec((d, LANE), fixed)],
        out_specs=(pl.BlockSpec((tm, d), row), pl.BlockSpec((tm, d // 2), row), pl.BlockSpec((tm, LANE), row)),
        compiler_params=_cparams(("arbitrary",)),
        name="ln_router",
    )(x, mix, g, b, w_router)


def _row_gather(n, src_hbm, src_row, dst, sem, wait):
    def body(g, carry):
        for u in range(SUBLANE):
            cp = pltpu.make_async_copy(src_hbm.at[pl.ds(src_row(g * SUBLANE + u), 1)], dst.at[g, pl.ds(u, 1)], sem)
            if wait:
                cp.wait()
            else:
                cp.start()
        return carry

    lax.fori_loop(0, n // SUBLANE, body, 0)


def _dispatch_body(tok_ref, meta_ref, xp_hbm, o_ref, buf, sems):
    b = pl.program_id(0)
    n_used = meta_ref[0]

    def rows(blk, wait):
        slot = blk % 2
        _row_gather(MOE_BLOCK, xp_hbm, lambda r: tok_ref[blk * MOE_BLOCK + r], buf.at[slot], sems.at[slot], wait)

    @pl.when((b == 0) & (n_used > 0))
    def _():
        rows(b, False)

    @pl.when(b + 1 < n_used)
    def _():
        rows(b + 1, False)

    @pl.when(b < n_used)
    def _():
        rows(b, True)
        o_ref[...] = buf[b % 2].reshape(o_ref.shape)

    @pl.when(b >= n_used)
    def _():
        o_ref[...] = jnp.zeros_like(o_ref)


def _dispatch(row_tok, meta, xp):
    dh = xp.shape[1]
    n_rows = row_tok.shape[0]
    return pl.pallas_call(
        _dispatch_body,
        out_shape=jax.ShapeDtypeStruct((n_rows, dh), U32),
        grid_spec=pltpu.PrefetchScalarGridSpec(
            num_scalar_prefetch=2,
            grid=(n_rows // MOE_BLOCK,),
            in_specs=[pl.BlockSpec(memory_space=pl.ANY)],
            out_specs=pl.BlockSpec((MOE_BLOCK, dh), lambda b, tok, meta: (b, 0)),
            scratch_shapes=[pltpu.VMEM((2, MOE_BLOCK // SUBLANE, SUBLANE, dh), U32), pltpu.SemaphoreType.DMA((2,))],
        ),
        compiler_params=_cparams(("arbitrary",)),
        name="moe_dispatch",
    )(row_tok, meta, xp)


def _weight_jobs(first_ref, rank_ref, dist_ref, meta_ref, w_hbms, layer, chunk, wbuf, sems):
    p, b = pl.program_id(0), pl.program_id(1)
    n_phase = pl.num_programs(0)
    nd = meta_ref[1]
    r = rank_ref[b]
    is_first = first_ref[b] == 1
    slot = (p * nd + r) & 1

    def copies(phase, rnk, slot_):
        e = dist_ref[rnk]
        col = pl.multiple_of(phase * chunk, chunk)
        return [pltpu.make_async_copy(w.at[layer, e, :, pl.ds(col, chunk)], wbuf.at[slot_, i], sems.at[slot_])
                for i, w in enumerate(w_hbms)]

    @pl.when(is_first & (p == 0) & (r == 0))
    def _():
        for cp in copies(p, r, slot):
            cp.start(priority=1)

    wrap = r + 1 >= nd
    nxt_p = jnp.where(wrap, p + 1, p)
    nxt_r = jnp.where(wrap, 0, r + 1)

    @pl.when(is_first & (nxt_p < n_phase))
    def _():
        for cp in copies(nxt_p, nxt_r, 1 - slot):
            cp.start(priority=1)

    @pl.when(is_first)
    def _():
        for cp in copies(p, r, slot):
            cp.wait()

    return is_first, slot


def _expert_up_body(first_ref, rank_ref, dist_ref, meta_ref, xs_ref, wg_hbm, wu_hbm, h_ref, wbuf, wgb, wub, sems,
                    *, layer, hc):
    is_first, slot = _weight_jobs(first_ref, rank_ref, dist_ref, meta_ref, (wg_hbm, wu_hbm), layer, hc, wbuf, sems)
    blk = pl.program_id(1)
    n_used = meta_ref[0]

    @pl.when(is_first)
    def _():
        wgb[...] = wbuf[slot, 0].astype(BF16)
        wub[...] = wbuf[slot, 1].astype(BF16)

    @pl.when(blk < n_used)
    def _():
        hi, lo = _unpack_bf16_pair(xs_ref[...])
        half = hi.shape[1]
        gate = (jnp.dot(hi, wgb[:half], preferred_element_type=F32)
                + jnp.dot(lo, wgb[half:], preferred_element_type=F32))
        up = (jnp.dot(hi, wub[:half], preferred_element_type=F32)
              + jnp.dot(lo, wub[half:], preferred_element_type=F32))
        h_ref[...] = (_silu(gate) * up).astype(BF16)

    @pl.when(blk >= n_used)
    def _():
        h_ref[...] = jnp.zeros_like(h_ref)


def _expert_up(plan, xs, w_gate, w_up, layer, hc):
    n_rows, dh = xs.shape
    d, hdim = w_gate.shape[2], w_gate.shape[3]
    n_blk = n_rows // MOE_BLOCK
    return pl.pallas_call(
        functools.partial(_expert_up_body, layer=layer, hc=hc),
        out_shape=jax.ShapeDtypeStruct((n_rows, hdim), BF16),
        grid_spec=pltpu.PrefetchScalarGridSpec(
            num_scalar_prefetch=4,
            grid=(hdim // hc, n_blk),
            in_specs=[pl.BlockSpec((MOE_BLOCK, dh), lambda p, b, *_: (b, 0)),
                      pl.BlockSpec(memory_space=pl.ANY), pl.BlockSpec(memory_space=pl.ANY)],
            out_specs=pl.BlockSpec((MOE_BLOCK, hc), lambda p, b, *_: (b, p)),
            scratch_shapes=[pltpu.VMEM((2, 2, d, hc), F32), pltpu.VMEM((d, hc), BF16), pltpu.VMEM((d, hc), BF16),
                            pltpu.SemaphoreType.DMA((2,))],
        ),
        compiler_params=_cparams(("arbitrary", "arbitrary")),
        name="expert_up",
    )(*plan, xs, w_gate, w_up)


def _expert_down_body(first_ref, rank_ref, dist_ref, meta_ref, h_ref, wd_hbm, y_ref, wbuf, wdb, sems, *, layer, oc):
    is_first, slot = _weight_jobs(first_ref, rank_ref, dist_ref, meta_ref, (wd_hbm,), layer, oc, wbuf, sems)
    blk = pl.program_id(1)
    n_used = meta_ref[0]

    @pl.when(is_first)
    def _():
        wdb[...] = wbuf[slot, 0].astype(BF16)

    @pl.when(blk < n_used)
    def _():
        y_ref[...] = jnp.dot(h_ref[...], wdb[...], preferred_element_type=F32)

    @pl.when(blk >= n_used)
    def _():
        y_ref[...] = jnp.zeros_like(y_ref)


def _expert_down(plan, h, w_down, layer, oc):
    n_rows, hdim = h.shape
    d = w_down.shape[3]
    n_blk = n_rows // MOE_BLOCK
    return pl.pallas_call(
        functools.partial(_expert_down_body, layer=layer, oc=oc),
        out_shape=jax.ShapeDtypeStruct((n_rows, d), F32),
        grid_spec=pltpu.PrefetchScalarGridSpec(
            num_scalar_prefetch=4,
            grid=(d // oc, n_blk),
            in_specs=[pl.BlockSpec((MOE_BLOCK, hdim), lambda q, b, *_: (b, 0)), pl.BlockSpec(memory_space=pl.ANY)],
            out_specs=pl.BlockSpec((MOE_BLOCK, oc), lambda q, b, *_: (b, q)),
            scratch_shapes=[pltpu.VMEM((2, 1, hdim, oc), F32), pltpu.VMEM((hdim, oc), BF16),
                            pltpu.SemaphoreType.DMA((2,))],
        ),
        compiler_params=_cparams(("arbitrary", "arbitrary")),
        name="expert_down",
    )(*plan, h, w_down)


def _combine_ln_body(dest_ref, x_ref, rt_ref, g_ref, b_ref, y_hbm, xo_ref, xb_ref, rows, sems, *, alpha, tm):
    s = pl.program_id(0)

    def gather(step, wait):
        slot = step % 2
        for k in range(MOE_TOPK):
            _row_gather(tm, y_hbm, lambda r: dest_ref[(step * tm + r) * MOE_TOPK + k], rows.at[slot, k],
                        sems.at[slot], wait)

    @pl.when(s == 0)
    def _():
        gather(s, False)

    @pl.when(s + 1 < pl.num_programs(0))
    def _():
        gather(s + 1, False)

    gather(s, True)
    slot = s % 2
    rt = rt_ref[...]
    x = x_ref[...]
    ffn = rt[:, 2:3] * rows[slot, 0].reshape(x.shape) + rt[:, 3:4] * rows[slot, 1].reshape(x.shape)
    y = _layer_norm(alpha * x + ffn, g_ref[...], b_ref[...])
    xo_ref[...] = y
    xb_ref[...] = y.astype(BF16)


def _combine_ln(dest, x, rt, g, b, y_rows, alpha, tm):
    t, d = x.shape
    row = lambda i, de: (i, 0)
    fixed = lambda i, de: (0, 0)
    return pl.pallas_call(
        functools.partial(_combine_ln_body, alpha=alpha, tm=tm),
        out_shape=(jax.ShapeDtypeStruct((t, d), F32), jax.ShapeDtypeStruct((t, d), BF16)),
        grid_spec=pltpu.PrefetchScalarGridSpec(
            num_scalar_prefetch=1,
            grid=(t // tm,),
            in_specs=[pl.BlockSpec((tm, d), row), pl.BlockSpec((tm, LANE), row), pl.BlockSpec((1, d), fixed),
                      pl.BlockSpec((1, d), fixed), pl.BlockSpec(memory_space=pl.ANY)],
            out_specs=(pl.BlockSpec((tm, d), row), pl.BlockSpec((tm, d), row)),
            scratch_shapes=[pltpu.VMEM((2, MOE_TOPK, tm // SUBLANE, SUBLANE, d), F32), pltpu.SemaphoreType.DMA((2,))],
        ),
        compiler_params=_cparams(("arbitrary",)),
        name="moe_combine_ln",
    )(dest, x, rt, g, b, y_rows)


def _moe_plan(rt, n_experts, n_blocks):
    n_assign = rt.shape[0] * MOE_TOPK
    e = rt[:, :MOE_TOPK].astype(I32).reshape(-1)
    onehot = (e[:, None] == jnp.arange(n_experts, dtype=I32)[None, :]).astype(I32)
    csum = jnp.cumsum(onehot, axis=0)
    rank_in_e = jnp.sum(csum * onehot, axis=1) - 1
    counts = csum[-1]
    padded = (counts + MOE_BLOCK - 1) // MOE_BLOCK * MOE_BLOCK
    pad_end = jnp.cumsum(padded)
    pad_start = pad_end - padded
    dest = (jnp.sum(onehot * pad_start[None, :], axis=1) + rank_in_e).astype(I32)
    row_tok = jnp.zeros((n_blocks * MOE_BLOCK,), I32).at[dest].set(jnp.arange(n_assign, dtype=I32) // MOE_TOPK)
    blk = jnp.arange(n_blocks, dtype=I32)
    blk_e = jnp.minimum(jnp.searchsorted(pad_end, blk * MOE_BLOCK, side="right"), n_experts - 1).astype(I32)
    n_used = (pad_end[-1] // MOE_BLOCK).astype(I32)
    prev_e = jnp.concatenate([jnp.full((1,), -1, I32), blk_e[:-1]])
    first = ((blk < n_used) & (blk_e != prev_e)).astype(I32)
    rank = jnp.maximum(jnp.cumsum(first) - 1, 0).astype(I32)
    dist_e = jnp.argsort(counts == 0, stable=True).astype(I32)
    meta = jnp.stack([n_used, jnp.sum(first).astype(I32)])
    return dest, row_tok, (first, rank, dist_e, meta)


def _pool_body(u_ref, st_ref, alias_ref, o_ref, ext, *, q, nb, start_pos, gd):
    del alias_ref
    c = pl.program_id(1)
    head = POOL_BUF + 1

    @pl.when(c == 0)
    def _():
        for i in range(nb):
            ext[i, 1:head] = st_ref[i]

    pos = (start_pos + c * q + lax.broadcasted_iota(I32, (q, 1), 0) + 1).astype(F32)
    for i in range(nb):
        ext[i, head:head + q] = u_ref[i * q:(i + 1) * q]
    for i in range(nb):
        for gi, w in enumerate(POOL_WINDOWS):
            sl = slice(gi * gd, (gi + 1) * gd)
            acc = ext[i, head:head + q, sl]
            cur = acc
            for k in range(1, w):
                acc = acc + ext[i, head - k:head - k + q, sl]
            pooled = acc / jnp.minimum(pos, float(w))
            o_ref[i * q:(i + 1) * q, sl] = (pooled - cur).astype(o_ref.dtype)
    for i in range(nb):
        ext[i, 0:head] = ext[i, q:q + head]


def _pool_mix(u, state, lead, out_buf, row0, n_tiles_outer, nc, q, nb, start_pos):
    t, cdim = u.shape
    gd = cdim // len(POOL_WINDOWS)
    rb0 = row0 // TILE
    nl = len(lead)
    rows = lambda b, c: (rb0 + b * nc + c, 0)
    return pl.pallas_call(
        functools.partial(_pool_body, q=q, nb=nb, start_pos=start_pos, gd=gd),
        out_shape=jax.ShapeDtypeStruct(out_buf.shape, out_buf.dtype),
        grid=(n_tiles_outer, nc),
        in_specs=[pl.BlockSpec((TILE, cdim), rows),
                  pl.BlockSpec((None,) * nl + (nb, POOL_BUF, cdim), lambda b, c: tuple(lead) + (b, 0, 0)),
                  pl.BlockSpec(memory_space=pl.ANY)],
        out_specs=pl.BlockSpec((TILE, cdim), rows),
        scratch_shapes=[pltpu.VMEM((nb, POOL_BUF + 1 + q, cdim), F32)],
        input_output_aliases={2: 0},
        compiler_params=_cparams(("arbitrary", "arbitrary")),
        name="pool_mix",
    )(u, state, out_buf)


def _ssd_body(xs_ref, bm_ref, cm_ref, z_ref, dt_ref, sx_ref, sb_ref, sc_ref, h0_ref,
              wx_ref, wb_ref, wc_ref, bx_ref, bb_ref, bc_ref, dtb_ref, alog_ref, dexp_ref, nw_ref, alias_ref,
              halias_ref, yb_ref, hout_ref, hs, ex, eb, ec, *, q, nb, hpg, kconv):
    del alias_ref, halias_ref
    c = pl.program_id(2)
    p_dim = SSM_HEAD_DIM
    off = SUBLANE - (kconv - 1)

    @pl.when(c == 0)
    def _():
        hs[...] = h0_ref[...].reshape(hs.shape)
        for i in range(nb):
            ex[i, off:SUBLANE] = sx_ref[i]
            eb[i, off:SUBLANE] = sb_ref[i]
            ec[i, off:SUBLANE] = sc_ref[i]

    def conv(raw_ref, e_ref, w_ref, bias_ref):
        outs = []
        for i in range(nb):
            e_ref[i, SUBLANE:SUBLANE + q] = raw_ref[i * q:(i + 1) * q]
            acc = bias_ref[...] + e_ref[i, off:off + q] * w_ref[0:1, :]
            for k in range(1, kconv):
                acc = acc + e_ref[i, off + k:off + k + q] * w_ref[k:k + 1, :]
            outs.append(acc)
            e_ref[i, 0:SUBLANE] = e_ref[i, q:q + SUBLANE]
        o = outs[0] if nb == 1 else jnp.concatenate(outs, axis=0)
        return _silu(o)

    xs = conv(xs_ref, ex, wx_ref, bx_ref)
    bm = conv(bm_ref, eb, wb_ref, bb_ref)
    cm = conv(cm_ref, ec, wc_ref, bc_ref)
    dt = _softplus(dt_ref[...] + dtb_ref[...])
    dta = dt * (-jnp.exp(alog_ref[...]))
    row = lax.broadcasted_iota(I32, (TILE, TILE), 0)
    col = lax.broadcasted_iota(I32, (TILE, TILE), 1)
    same = (row // q) == (col // q)
    causal = same & (col <= row)
    hp = lax.Precision.HIGHEST
    acs = jnp.dot(causal.astype(F32), dta, preferred_element_type=F32, precision=hp)
    tot = jnp.dot(same.astype(F32), dta, preferred_element_type=F32, precision=hp)
    acs_t = acs.T
    tot_t = tot.T
    e_acs = jnp.exp(acs)
    to_end = jnp.exp(tot - acs)
    bm_b = bm.astype(BF16)
    cm_b = cm.astype(BF16)
    scores = lax.dot_general(cm_b, bm_b, _NT, preferred_element_type=F32)
    lo_half = lax.broadcasted_iota(I32, (TILE, LANE), 1) < p_dim

    y_parts, xw_t_parts, ea_parts = [], [], []
    for p in range(hpg // 2):
        j0, j1 = 2 * p, 2 * p + 1

        def expand(v):
            return jnp.where(lo_half, v[:, j0:j0 + 1], v[:, j1:j1 + 1])

        xs_p = xs[:, p * LANE:(p + 1) * LANE]
        xdt = xs_p * expand(dt)
        xdt_b = xdt.astype(BF16)
        ys = []
        for j in (j0, j1):
            decay = jnp.exp(jnp.where(causal, acs[:, j:j + 1] - acs_t[j:j + 1, :], -jnp.inf))
            ys.append(jnp.dot((scores * decay).astype(BF16), xdt_b, preferred_element_type=F32))
        y_parts.append(jnp.where(lo_half, ys[0], ys[1]))
        xw_t_parts.append((xdt * expand(to_end)).T.astype(BF16))
        ea_parts.append(expand(e_acs))
    y_diag = jnp.concatenate(y_parts, axis=1)
    xw_t = jnp.concatenate(xw_t_parts, axis=0)
    ea = jnp.concatenate(ea_parts, axis=1)

    sub = lax.broadcasted_iota(I32, (TILE, 1), 0) // q
    y_off = None
    for i in range(nb):
        h_i = hs[i]
        full = lax.dot_general(cm_b, h_i.astype(BF16), _NT, preferred_element_type=F32)
        y_off = full if nb == 1 else (jnp.where(sub == i, full, 0.0) + (0.0 if y_off is None else y_off))
        bm_i = bm_b if nb == 1 else jnp.where(sub == i, bm, 0.0).astype(BF16)
        st = jnp.dot(xw_t, bm_i, preferred_element_type=F32)
        for j in range(hpg):
            rs = slice(j * p_dim, (j + 1) * p_dim)
            cd = jnp.exp(tot_t[j:j + 1, i * q:i * q + 1])
            hs[i, rs, :] = h_i[rs, :] * cd + st[rs, :]

    y = y_diag + y_off * ea + xs * dexp_ref[...]
    y = y * _silu(z_ref[...])
    ms = jnp.mean(y * y, axis=1, keepdims=True)
    yb_ref[...] = (y * lax.rsqrt(ms + RMS_EPS) * nw_ref[...]).astype(yb_ref.dtype)

    @pl.when(c == pl.num_programs(2) - 1)
    def _():
        hout_ref[...] = hs[...].reshape(hout_ref.shape)


def _ssd_mix(xbc, z, dtr, conv_state, conv_lead, h0, h0_lead, conv_w, conv_b, dtb, alog, dexp, nw, layer,
             out_buf, h_buf, row0, n_outer, nc, q, nb, inner):
    n_state = LANE
    p_dim = SSM_HEAD_DIM
    gw = inner // SSM_GROUPS
    hpg = gw // SSM_HEAD_DIM
    assert hpg % 2 == 0 and gw % LANE == 0
    kconv = conv_w.shape[1]
    rb0 = row0 // TILE
    b_blk = inner // n_state
    c_blk = (inner + SSM_GROUPS * n_state) // n_state
    ncl, nhl = len(conv_lead), len(h0_lead)

    def rows(colf):
        return lambda b, g, c: (rb0 + b * nc + c, colf(g))

    def cstate(width, colf):
        return pl.BlockSpec((None,) * ncl + (nb, kconv - 1, width), lambda b, g, c: tuple(conv_lead) + (b, 0, colf(g)))

    def wspec(nrows, width, colf):
        return pl.BlockSpec((None, nrows, width), lambda b, g, c: (layer, 0, colf(g)))

    xcol = lambda g: g
    bcol = lambda g: b_blk + g
    ccol = lambda g: c_blk + g
    in_specs = [
        pl.BlockSpec((TILE, gw), rows(xcol)), pl.BlockSpec((TILE, n_state), rows(bcol)),
        pl.BlockSpec((TILE, n_state), rows(ccol)), pl.BlockSpec((TILE, gw), rows(xcol)),
        pl.BlockSpec((TILE, LANE), rows(xcol)),
        cstate(gw, xcol), cstate(n_state, bcol), cstate(n_state, ccol),
        pl.BlockSpec((None,) * nhl + (nb, hpg, p_dim, n_state), lambda b, g, c: tuple(h0_lead) + (b, g, 0, 0)),
        wspec(kconv, gw, xcol), wspec(kconv, n_state, bcol), wspec(kconv, n_state, ccol),
        wspec(1, gw, xcol), wspec(1, n_state, bcol), wspec(1, n_state, ccol),
        wspec(1, LANE, xcol), wspec(1, LANE, xcol), wspec(1, gw, xcol), wspec(1, gw, xcol),
        pl.BlockSpec(memory_space=pl.ANY), pl.BlockSpec(memory_space=pl.ANY),
    ]
    return pl.pallas_call(
        functools.partial(_ssd_body, q=q, nb=nb, hpg=hpg, kconv=kconv),
        out_shape=(jax.ShapeDtypeStruct(out_buf.shape, out_buf.dtype),
                   jax.ShapeDtypeStruct(h_buf.shape, h_buf.dtype)),
        grid=(n_outer, SSM_GROUPS, nc),
        in_specs=in_specs,
        out_specs=(pl.BlockSpec((TILE, gw), rows(xcol)),
                   pl.BlockSpec((None, nb, hpg, p_dim, n_state), lambda b, g, c: (layer, b, g, 0, 0))),
        scratch_shapes=[pltpu.VMEM((nb, gw, n_state), F32), pltpu.VMEM((nb, SUBLANE + q, gw), F32),
                        pltpu.VMEM((nb, SUBLANE + q, n_state), F32), pltpu.VMEM((nb, SUBLANE + q, n_state), F32)],
        input_output_aliases={19: 0, 20: 1},
        compiler_params=_cparams(("arbitrary", "arbitrary", "arbitrary")),
        name="ssd_mix",
    )(xbc, xbc, xbc, z, dtr, conv_state, conv_state, conv_state, h0, conv_w, conv_w, conv_w, conv_b, conv_b, conv_b,
      dtb, alog, dexp, nw, out_buf, h_buf)


def _cmlp_body(u_ref, v_ref, ws_ref, bs_ref, g_ref, b_ref, alias_ref, *out_refs, q, n_heads, emit_v):
    del alias_ref
    o_ref = out_refs[0]
    vn = _layer_norm(v_ref[...], g_ref[...], b_ref[...])
    if emit_v:
        out_refs[1][...] = vn
    vb = vn.astype(BF16)
    row = lax.broadcasted_iota(I32, (TILE, TILE), 0)
    col = lax.broadcasted_iota(I32, (TILE, TILE), 1)
    causal = ((row // q) == (col // q)) & (col <= row)
    hd = vb.shape[1] // n_heads
    bs = bs_ref[...]
    for h in range(n_heads):
        sl = slice(h * hd, (h + 1) * hd)
        wm = jnp.where(causal, ws_ref[h], 0.0).astype(BF16)
        mixed = jnp.dot(wm, vb[:, sl], preferred_element_type=F32) + bs[:, h:h + 1]
        o_ref[:, sl] = (u_ref[:, sl].astype(F32) * mixed).astype(o_ref.dtype)


def _cmlp_mix(u, v, ws, ws_lead, bs_t, vg, vb, out_buf, row0, n_tiles, q, emit_v):
    t, cdim = u.shape
    n_heads = ws.shape[-3]
    rb0 = row0 // TILE
    nl = len(ws_lead)
    rows = lambda i: (rb0 + i, 0)
    fixed = lambda i: (0, 0)
    out_shape = [jax.ShapeDtypeStruct(out_buf.shape, out_buf.dtype)]
    out_specs = [pl.BlockSpec((TILE, cdim), rows)]
    if emit_v:
        out_shape.append(jax.ShapeDtypeStruct((n_tiles * TILE, cdim), F32))
        out_specs.append(pl.BlockSpec((TILE, cdim), lambda i: (i, 0)))
    return pl.pallas_call(
        functools.partial(_cmlp_body, q=q, n_heads=n_heads, emit_v=emit_v),
        out_shape=tuple(out_shape),
        grid=(n_tiles,),
        in_specs=[pl.BlockSpec((TILE, cdim), rows), pl.BlockSpec((TILE, cdim), rows),
                  pl.BlockSpec((None,) * nl + (n_heads, TILE, TILE), lambda i: tuple(ws_lead) + (0, 0, 0)),
                  pl.BlockSpec((TILE, LANE), fixed), pl.BlockSpec((1, cdim), fixed), pl.BlockSpec((1, cdim), fixed),
                  pl.BlockSpec(memory_space=pl.ANY)],
        out_specs=tuple(out_specs),
        input_output_aliases={6: 0},
        compiler_params=_cparams(("arbitrary",)),
        name="cmlp_mix",
    )(u, v, ws, bs_t, vg, vb, out_buf)


def _pick(n, target):
    t = min(n, target)
    while n % t:
        t -= SUBLANE
    return t


def kernel(x_prompt, x_sample, state_pool, state_conv, state_ssm, w_in_even, pool_w, pool_scale, conv_w, conv_b, dt_bias, a_log, d_skip, ssm_norm_w, w_out_even, cmlp_w_in, cmlp_b_in, cmlp_v_g, cmlp_v_b, cmlp_w_s, cmlp_b_s, cmlp_w_out, ln_mix_g, ln_mix_b, ln_ffn_g, ln_ffn_b, moe_w_group, moe_w_expert, moe_w_gate, moe_w_up, moe_w_down):
    bp, lp, d = x_prompt.shape
    bs, ls, _ = x_sample.shape
    depth = ln_mix_g.shape[0]
    alpha = float((2 * depth) ** 0.25)
    tp, ts = bp * lp, bs * ls
    t = tp + ts
    pool_dim = state_pool.shape[-1]
    conv_dim = state_conv.shape[-1]
    n_heads_ssm = state_ssm.shape[2]
    inner = n_heads_ssm * SSM_HEAD_DIM
    n_state = state_ssm.shape[-1]
    hpg = n_heads_ssm // SSM_GROUPS
    n_groups = moe_w_group.shape[-1]
    n_experts = moe_w_expert.shape[-1]
    per_group = n_experts // n_groups
    cdim = cmlp_w_out.shape[1]
    c_heads = cmlp_w_s.shape[1]
    assert n_state == LANE and SEQ_CHUNK == TILE and lp % TILE == 0 and TILE % ls == 0 and ts % TILE == 0
    assert n_groups + n_experts <= LANE and hpg <= LANE
    q_s = math.gcd(ls, SEQ_CHUNK)
    assert q_s == ls
    nb_s = TILE // q_s
    n_blocks = -(-(t * MOE_TOPK) // MOE_BLOCK) + n_experts

    tm = _pick(t, 1024)
    tm_ln = _pick(t, 256)
    tn = 512

    x = jnp.concatenate([x_prompt.reshape(tp, d), x_sample.reshape(ts, d)], axis=0)
    xb = x.astype(BF16)

    dt_col = pool_dim + inner + conv_dim

    def group_pad(v):
        lead = v.shape[:-1]
        v = v.reshape(lead + (SSM_GROUPS, hpg))
        v = jnp.pad(v, [(0, 0)] * len(lead) + [(0, 0), (0, LANE - hpg)])
        return v.reshape(lead + (SSM_GROUPS * LANE,))

    w_in_t = jnp.swapaxes(w_in_even, 1, 2)
    w_dt_t = w_in_t[:, dt_col:].reshape(-1, SSM_GROUPS, hpg, d)
    w_dt_t = jnp.pad(w_dt_t, ((0, 0), (0, 0), (0, LANE - hpg), (0, 0))).reshape(-1, SSM_GROUPS * LANE, d)
    dtb_p = group_pad(dt_bias)[:, None, :]
    alog_p = group_pad(a_log)[:, None, :]
    dexp = jnp.repeat(d_skip, SSM_HEAD_DIM, axis=-1)[:, None, :]
    nw3 = ssm_norm_w[:, None, :]
    conv_b3 = conv_b[:, None, :]
    w_router = jnp.concatenate(
        [moe_w_group, moe_w_expert, jnp.zeros((depth, d, LANE - n_groups - n_experts), F32)], axis=-1).astype(BF16)
    zero_pool = jnp.zeros((bp, POOL_BUF, pool_dim), F32)
    zero_conv = jnp.zeros((bp, state_conv.shape[2], conv_dim), F32)
    n_even = state_ssm.shape[0]
    zero_ssm = jnp.zeros((bp,) + state_ssm.shape[2:], F32)
    ssm_p = jnp.zeros((n_even, bp) + state_ssm.shape[2:], F32)
    ssm_s = jnp.zeros(state_ssm.shape, F32)
    eye = jnp.eye(nb_s, dtype=F32)
    ws_s = jnp.einsum("ab,lhts->lhatbs", eye, cmlp_w_s[:, :, :q_s, :q_s]).reshape(-1, c_heads, TILE, TILE)
    bs_p = jnp.pad(jnp.swapaxes(cmlp_b_s, 1, 2), ((0, 0), (0, 0), (0, LANE - c_heads)))
    bs_s = jnp.tile(bs_p[:, :q_s], (1, nb_s, 1))

    def last_rows(a, row0, n_seq, seq_len, n):
        return jnp.stack([a[row0 + (s + 1) * seq_len - n:row0 + (s + 1) * seq_len] for s in range(n_seq)])

    def new_state(old, a, n):
        cur = a[tp:].reshape(bs, ls, a.shape[1])
        ext = cur if ls >= n else jnp.concatenate([old[:, ls:], cur], axis=1)
        return last_rows(a, 0, bp, lp, n), ext[:, ext.shape[1] - n:]

    pool_out, conv_out, v_out = [], [], []
    for layer in range(depth):
        i = layer // 2
        if layer % 2 == 0:
            lhs = [(xb, 0)]
            u_a = _matmul(lhs, w_in_t, (i,), (0,), d, 0, pool_dim, tn, tm, F32, name="in_pool", w_t=True)
            z = _matmul(lhs, w_in_t, (i,), (0,), d, pool_dim, inner, tn, tm, F32, name="in_gate", w_t=True)
            xbc = _matmul(lhs, w_in_t, (i,), (0,), d, pool_dim + inner, conv_dim, tn, tm, F32, name="in_conv",
                          w_t=True)
            dtr = _matmul(lhs, w_dt_t, (i,), (0,), d, 0, SSM_GROUPS * LANE, tn, tm, F32, name="in_dt", w_t=True)

            mixed = jnp.zeros((t, pool_dim), BF16)
            mixed = _pool_mix(u_a, zero_pool, (), mixed, 0, bp, lp // TILE, TILE, 1, 0)
            mixed = _pool_mix(u_a, state_pool, (i,), mixed, tp, ts // TILE, 1, q_s, nb_s, PAST_LEN)
            y_a = _group_matmul(mixed, pool_w, i, pool_scale[i][None, :], tm)

            y_b = jnp.zeros((t, inner), BF16)
            y_b, ssm_p = _ssd_mix(xbc, z, dtr, zero_conv, (), zero_ssm, (), conv_w, conv_b3, dtb_p, alog_p, dexp, nw3,
                                  i, y_b, ssm_p, 0, bp, lp // TILE, TILE, 1, inner)
            y_b, ssm_s = _ssd_mix(xbc, z, dtr, state_conv, (i,), state_ssm, (i,), conv_w, conv_b3, dtb_p, alog_p,
                                  dexp, nw3, i, y_b, ssm_s, tp, ts // TILE, 1, q_s, nb_s, inner)

            kc = pool_dim
            assert inner % kc == 0
            lhs_out = [(y_a, 0)] + [(y_b, k) for k in range(inner // kc)]
            mix = _matmul(lhs_out, w_out_even, (i,), tuple(range(1 + inner // kc)), kc, 0, d, tn, _pick(t, 256), F32,
                          name="out_even")

            pool_out.append(new_state(state_pool[i], u_a, POOL_BUF))
            conv_out.append(new_state(state_conv[i], xbc, state_conv.shape[2]))
        else:
            lhs = [(xb, 0)]
            bias = cmlp_b_in[i][None, :]
            u = _matmul(lhs, cmlp_w_in, (i,), (0,), d, 0, cdim, tn, tm, F32, bias=bias[:, :cdim], act="gelu",
                        name="cmlp_in_u")
            v = _matmul(lhs, cmlp_w_in, (i,), (0,), d, cdim, cdim, tn, tm, F32, bias=bias[:, cdim:], act="gelu",
                        name="cmlp_in_v")
            vg, vb_ = cmlp_v_g[i][None, :], cmlp_v_b[i][None, :]
            gated = jnp.zeros((t, cdim), BF16)
            (gated,) = _cmlp_mix(u, v, cmlp_w_s, (i,), bs_p[i], vg, vb_, gated, 0, tp // TILE, SEQ_CHUNK, False)
            gated, v_n = _cmlp_mix(u, v, ws_s, (i,), bs_s[i], vg, vb_, gated, tp, ts // TILE, q_s, True)
            v_out.append(v_n.reshape(bs, ls, cdim))
            mix = _matmul([(gated, 0)], cmlp_w_out, (i,), (0,), cdim, 0, d, tn, tm, F32, name="cmlp_out")

        x, xp, rt = _ln_router(x, mix, ln_mix_g[layer][None, :], ln_mix_b[layer][None, :], w_router[layer], alpha,
                               n_groups, per_group, tm_ln)
        dest, row_tok, plan = _moe_plan(rt, n_experts, n_blocks)
        xs = _dispatch(row_tok, plan[3], xp)
        hdim = moe_w_gate.shape[-1]
        h = _expert_up(plan, xs, moe_w_gate, moe_w_up, layer, min(hdim, 512))
        y_rows = _expert_down(plan, h, moe_w_down, layer, min(d, 2048))
        x, xb = _combine_ln(dest, x, rt, ln_ffn_g[layer][None, :], ln_ffn_b[layer][None, :], y_rows, alpha, tm_ln)

    y_prompt = x[:tp].reshape(bp, lp, d)
    y_sample = x[tp:].reshape(bs, ls, d)
    pool_p = jnp.stack([a for a, _ in pool_out])
    pool_s = jnp.stack([b for _, b in pool_out])
    conv_p = jnp.stack([a for a, _ in conv_out])
    conv_s = jnp.stack([b for _, b in conv_out])
    v_s = jnp.stack(v_out)
    return (y_prompt, y_sample, pool_p, pool_s, conv_p, conv_s, ssm_p, ssm_s, v_s)
```
